```python
import math
import jax
import jax.numpy as jnp
from jax import lax
import numpy as np


D_MODEL = 1024
BATCH = 8
SEQ = 4096
DEPTH = 4

GRID_W = 64
CTX_LEN = 256

NA_HEADS = 8
NA_HEAD_DIM = 64
NA_WIN_ROWS = 8
NA_WIN_COLS = 16
NA_RPB_ROWS = 2 * NA_WIN_ROWS - 1
NA_RPB_COLS = 2 * NA_WIN_COLS - 1

GQA_HEADS = 8
GQA_KV_HEADS = 2
GQA_HEAD_DIM = 64
Q_BLOCK = 128
ROPE_THETA = 10000.0

HY_WIDTH = 512
HY_SHORT = 3
HY_FILTER_BANDS = 8
HY_FILTER_EMB = 1 + 2 * HY_FILTER_BANDS
HY_FILTER_HIDDEN = 64
HY_FAST_DECAY = 0.3
HY_SLOW_DECAY = 1.5
HY_DECAY_TARGET = 1e-2

PEER_HEADS = 8
PEER_NKEYS = 128
PEER_EXPERTS = PEER_NKEYS * PEER_NKEYS
PEER_DKEY = 256
PEER_TOPK = 16
PEER_CHUNK = 128

N_BRANCHES = 3
NA_WIDTH = NA_HEADS * NA_HEAD_DIM
GQA_Q_WIDTH = GQA_HEADS * GQA_HEAD_DIM
GQA_KV_WIDTH = GQA_KV_HEADS * GQA_HEAD_DIM
IN_SPLITS = (NA_WIDTH, NA_WIDTH, NA_WIDTH, GQA_Q_WIDTH, GQA_KV_WIDTH, GQA_KV_WIDTH, 3 * HY_WIDTH, N_BRANCHES * D_MODEL)
IN_COLS = sum(IN_SPLITS)

ALPHA = (2 * DEPTH) ** 0.25
BETA = (8 * DEPTH) ** -0.25
LN_EPS = 1e-5
RMS_EPS = 1e-6

kernel_name = 'hybrid_na_gqa_hyena_peer_dit'


def layer_norm(x, g, b):
    xf = x.astype(jnp.float32)
    mu = jnp.mean(xf, axis=-1, keepdims=True)
    var = jnp.mean(jnp.square(xf - mu), axis=-1, keepdims=True)
    return ((xf - mu) * lax.rsqrt(var + LN_EPS) * g + b).astype(x.dtype)


def rms_norm(x, g):
    xf = x.astype(jnp.float32)
    return (xf * lax.rsqrt(jnp.mean(xf * xf, axis=-1, keepdims=True) + RMS_EPS) * g).astype(x.dtype)


def softmax_f32(s):
    return jax.nn.softmax(s.astype(jnp.float32), axis=-1)


def split_heads(t, n):
    return t.reshape(t.shape[0], t.shape[1], n, t.shape[-1] // n)


def split_cols(z):
    offs = np.cumsum(IN_SPLITS)[:-1].tolist()
    return jnp.split(z, offs, axis=-1)


def rotate_half(x):
    x1, x2 = jnp.split(x, 2, axis=-1)
    return jnp.concatenate([-x2, x1], axis=-1)


def axial_rope(x):
    T, dh = x.shape[1], x.shape[-1]
    half = dh // 2
    t = jnp.arange(T)
    inv = ROPE_THETA ** (-jnp.arange(0, half, 2, dtype=jnp.float32) / half)

    def rot(xp, pos):
        ang = pos.astype(jnp.float32)[:, None] * inv[None, :]
        ang = jnp.concatenate([ang, ang], axis=-1)[None, :, None, :]
        xf = xp.astype(jnp.float32)
        return xf * jnp.cos(ang) + rotate_half(xf) * jnp.sin(ang)

    out = jnp.concatenate([rot(x[..., :half], t // GRID_W), rot(x[..., half:], t % GRID_W)], axis=-1)
    return out.astype(x.dtype)


def dense_attention(q, k, v):
    B, L, Hq, dh = q.shape
    Hkv = k.shape[2]
    qg = q.reshape(B, L, Hkv, Hq // Hkv, dh)
    s = jnp.einsum('bqkgd,bskd->bkgqs', qg, k).astype(jnp.float32) * (dh ** -0.5)
    p = softmax_f32(s).astype(v.dtype)
    return jnp.einsum('bkgqs,bskd->bqkgd', p, v).reshape(B, L, Hq * dh)


def neighbourhood_attention(q, k, v, kc, vc, rpb):
    B, T, H, dh = q.shape
    rows = T // GRID_W
    wr = min(NA_WIN_ROWS, rows)
    wc = NA_WIN_COLS
    scale = dh ** -0.5
    qg = q.reshape(B, rows, GRID_W, H, dh)
    kg = k.reshape(B, rows, GRID_W, H, dh)
    vg = v.reshape(B, rows, GRID_W, H, dh)
    cols = jnp.arange(GRID_W)
    col_idx = jnp.clip(cols - wc // 2, 0, GRID_W - wc)[:, None] + jnp.arange(wc)[None, :]
    dc = col_idx - cols[:, None] + (NA_WIN_COLS - 1)

    def row_block(r):
        r0 = jnp.clip(r - wr // 2, 0, rows - wr)
        qr = lax.dynamic_index_in_dim(qg, r, axis=1, keepdims=False)
        kw = jnp.take(lax.dynamic_slice_in_dim(kg, r0, wr, axis=1), col_idx, axis=2)
        vw = jnp.take(lax.dynamic_slice_in_dim(vg, r0, wr, axis=1), col_idx, axis=2)
        dr = r0 + jnp.arange(wr) - r + (NA_WIN_ROWS - 1)
        bias = rpb[:, dr[None, :, None], dc[:, None, :]]
        s_loc = jnp.einsum('bqhd,bpqjhd->bhqpj', qr, kw).astype(jnp.float32) * scale + bias[None]
        s_ctx = jnp.einsum('bqhd,bchd->bhqc', qr, kc).astype(jnp.float32) * scale
        p = softmax_f32(jnp.concatenate([s_loc.reshape(B, H, GRID_W, wr * wc), s_ctx], axis=-1)).astype(v.dtype)
        p_loc = p[..., :wr * wc].reshape(B, H, GRID_W, wr, wc)
        p_ctx = p[..., wr * wc:]
        return jnp.einsum('bhqpj,bpqjhd->bqhd', p_loc, vw) + jnp.einsum('bhqc,bchd->bqhd', p_ctx, vc)

    o = lax.map(row_block, jnp.arange(rows))
    return jnp.moveaxis(o, 0, 1).reshape(B, T, H * dh)


def gqa_blocked(q, k, v, kc, vc):
    B, T, Hq, dh = q.shape
    Hkv = k.shape[2]
    G = Hq // Hkv
    scale = dh ** -0.5
    k_all = jnp.concatenate([k, kc], axis=1)
    v_all = jnp.concatenate([v, vc], axis=1)
    qb = jnp.moveaxis(q.reshape(B, T // Q_BLOCK, Q_BLOCK, Hkv, G, dh), 1, 0)

    def block(qi):
        s = jnp.einsum('bqkgd,bskd->bkgqs', qi, k_all).astype(jnp.float32) * scale
        p = softmax_f32(s).astype(v.dtype)
        return jnp.einsum('bkgqs,bskd->bqkgd', p, v_all)

    o = lax.map(block, qb)
    return jnp.moveaxis(o, 0, 1).reshape(B, T, Hq * dh)


def short_conv(x, w, b):
    ch = x.shape[-1]
    y = lax.conv_general_dilated(x, w[:, None, :].astype(x.dtype), window_strides=(1,),
                                 padding=[(HY_SHORT // 2, HY_SHORT // 2)],
                                 dimension_numbers=('NWC', 'WIO', 'NWC'), feature_group_count=ch)
    return y + b


def hyena_filter(L, w1, b1, fr1, w2, b2, fr2, w3, b3):
    f32 = jnp.float32
    t = jnp.linspace(0.0, 1.0, L, dtype=f32)
    w = 2.0 * math.pi * jnp.arange(L, dtype=f32) / L
    bands = jnp.linspace(1e-4, HY_FILTER_BANDS - 1, HY_FILTER_BANDS, dtype=f32)
    z = jnp.concatenate([t[:, None], jnp.cos(w[:, None] * bands[None]), -jnp.sin(w[:, None] * bands[None])], axis=-1)
    hdn = jnp.sin(fr1 * (z @ w1 + b1))
    hdn = jnp.sin(fr2 * (hdn @ w2 + b2))
    k = (hdn @ w3 + b3).astype(f32).reshape(L, 2, HY_WIDTH)
    max_decay = math.log(HY_DECAY_TARGET) / HY_FAST_DECAY
    min_decay = math.log(HY_DECAY_TARGET) / HY_SLOW_DECAY
    deltas = jnp.linspace(min_decay, max_decay, HY_WIDTH, dtype=f32)
    decay = jnp.exp(-t[:, None] * jnp.abs(deltas)[None, :])
    k = k * decay[:, None, :]
    kfull = jnp.concatenate([k[:, 0], jnp.zeros((1, HY_WIDTH), f32), k[:0:-1, 1]], axis=0)
    return kfull / jnp.sum(jnp.abs(kfull), axis=0, keepdims=True)


def bidir_fftconv(u, kfull, bias):
    L = u.shape[1]
    uf = jnp.fft.rfft(u.astype(jnp.float32), n=2 * L, axis=1)
    kf = jnp.fft.rfft(kfull, n=2 * L, axis=0)
    y = jnp.fft.irfft(uf * kf[None], n=2 * L, axis=1)[:, :L]
    return (y + u.astype(jnp.float32) * bias).astype(u.dtype)


def hyena_mixer(zh, short_w, short_b, kfull, hy_bias):
    u = short_conv(zh, short_w, short_b)
    x0, x1, v = jnp.split(u, 3, axis=-1)
    return bidir_fftconv(v * x1, kfull, hy_bias) * x0


def merge_branches(ya, yb, yc, gate_logits, w_br_a, w_br_b, w_br_c, w_out):
    g = jax.nn.sigmoid(gate_logits.astype(jnp.float32)).astype(ya.dtype)
    ga, gb, gc = jnp.split(g, N_BRANCHES, axis=-1)
    m = ga * (ya @ w_br_a) + gb * (yb @ w_br_b) + gc * (yc @ w_br_c)
    return m @ w_out


def mixer_sublayer(h, hc, need_ctx, w_in, rpb, q_gain, k_gain, short_w, short_b,
                   fw1, fb1, ffr1, fw2, fb2, ffr2, fw3, fb3, hy_bias,
                   w_br_a, w_br_b, w_br_c, w_out):
    T = h.shape[1]
    C = hc.shape[1]
    qa, ka, va, qb, kb, vb, zh, gl = split_cols(h @ w_in)
    qac, kac, vac, qbc, kbc, vbc, zhc, glc = split_cols(hc @ w_in)
    kac_h = split_heads(kac, NA_HEADS)
    vac_h = split_heads(vac, NA_HEADS)
    kbc_h = rms_norm(split_heads(kbc, GQA_KV_HEADS), k_gain)
    vbc_h = split_heads(vbc, GQA_KV_HEADS)
    ya = neighbourhood_attention(split_heads(qa, NA_HEADS), split_heads(ka, NA_HEADS),
                                 split_heads(va, NA_HEADS), kac_h, vac_h, rpb)
    qb_h = axial_rope(rms_norm(split_heads(qb, GQA_HEADS), q_gain))
    kb_h = axial_rope(rms_norm(split_heads(kb, GQA_KV_HEADS), k_gain))
    yb = gqa_blocked(qb_h, kb_h, split_heads(vb, GQA_KV_HEADS), kbc_h, vbc_h)
    yc = hyena_mixer(zh, short_w, short_b, hyena_filter(T, fw1, fb1, ffr1, fw2, fb2, ffr2, fw3, fb3), hy_bias)
    out = merge_branches(ya, yb, yc, gl, w_br_a, w_br_b, w_br_c, w_out)
    if not need_ctx:
        return out, None
    yac = dense_attention(split_heads(qac, NA_HEADS), kac_h, vac_h)
    ybc = dense_attention(rms_norm(split_heads(qbc, GQA_HEADS), q_gain), kbc_h, vbc_h)
    ycc = hyena_mixer(zhc, short_w, short_b, hyena_filter(C, fw1, fb1, ffr1, fw2, fb2, ffr2, fw3, fb3), hy_bias)
    out_c = merge_branches(yac, ybc, ycc, glc, w_br_a, w_br_b, w_br_c, w_out)
    return out, out_c


def peer(h, w_q, sub_keys, u_tab, v_tab):
    B, L, D = h.shape
    n = B * L
    K = PEER_TOPK
    hf = h.reshape(n, D)
    q = (hf @ w_q).reshape(n, PEER_HEADS, 2, PEER_DKEY // 2)
    s = jnp.einsum('nhpd,pkd->nhpk', q, sub_keys).astype(jnp.float32)
    s_top, i_top = lax.top_k(s, K)
    cand = s_top[:, :, 0, :, None] + s_top[:, :, 1, None, :]
    cand_idx = i_top[:, :, 0, :, None] * PEER_NKEYS + i_top[:, :, 1, None, :]
    best, pos = lax.top_k(cand.reshape(n, PEER_HEADS, K * K), K)
    idx = jnp.take_along_axis(cand_idx.reshape(n, PEER_HEADS, K * K), pos, axis=-1)
    g = softmax_f32(best).astype(h.dtype)
    nc = n // PEER_CHUNK

    def chunk(args):
        hc, ic, gc = args
        u = jnp.take(u_tab, ic, axis=0)
        a = jax.nn.gelu(jnp.einsum('cd,chkd->chk', hc, u), approximate=False)
        v = jnp.take(v_tab, ic, axis=0)
        return jnp.einsum('chk,chkd->cd', a * gc, v)

    out = lax.map(chunk, (hf.reshape(nc, PEER_CHUNK, D), idx.reshape(nc, PEER_CHUNK, PEER_HEADS, K),
                          g.reshape(nc, PEER_CHUNK, PEER_HEADS, K)))
    return out.reshape(B, L, D)


def setup_inputs(seed: int = 0) -> dict:
    key = jax.random.key(seed)
    ks = iter(jax.random.split(key, 40))

    def nrm(shape, s):
        return jax.random.normal(next(ks), shape, jnp.float32) * s

    Dp = DEPTH
    D = D_MODEL
    return {
        'x': nrm((BATCH, SEQ, D), 1.0),
        'c': nrm((BATCH, D), 1.0),
        'ctx': nrm((BATCH, CTX_LEN, D), 1.0),
        'c_ctx': nrm((D,), 1.0),
        'w_ada': nrm((Dp, D, 6 * D), 0.5 * D ** -0.5),
        'b_ada': nrm((Dp, 6 * D), 0.01),
        'w_in': nrm((Dp, D, IN_COLS), D ** -0.5),
        'na_rpb': nrm((Dp, NA_HEADS, NA_RPB_ROWS, NA_RPB_COLS), 0.02),
        'gqa_q_gain': 1.0 + nrm((Dp, GQA_HEAD_DIM), 0.02),
        'gqa_k_gain': 1.0 + nrm((Dp, GQA_HEAD_DIM), 0.02),
        'hy_short_w': nrm((Dp, HY_SHORT, 3 * HY_WIDTH), HY_SHORT ** -0.5),
        'hy_short_b': nrm((Dp, 3 * HY_WIDTH), 0.01),
        'hy_f_w1': nrm((Dp, HY_FILTER_EMB, HY_FILTER_HIDDEN), HY_FILTER_EMB ** -0.5),
        'hy_f_b1': nrm((Dp, HY_FILTER_HIDDEN), 0.01),
        'hy_f_freq1': 1.0 + nrm((Dp, HY_FILTER_HIDDEN), 0.02),
        'hy_f_w2': nrm((Dp, HY_FILTER_HIDDEN, HY_FILTER_HIDDEN), HY_FILTER_HIDDEN ** -0.5),
        'hy_f_b2': nrm((Dp, HY_FILTER_HIDDEN), 0.01),
        'hy_f_freq2': 1.0 + nrm((Dp, HY_FILTER_HIDDEN), 0.02),
        'hy_f_w3': nrm((Dp, HY_FILTER_HIDDEN, 2 * HY_WIDTH), HY_FILTER_HIDDEN ** -0.5),
        'hy_f_b3': nrm((Dp, 2 * HY_WIDTH), 0.01),
        'hy_bias': nrm((Dp, HY_WIDTH), 1.0),
        'w_br_a': nrm((Dp, NA_WIDTH, D), BETA * NA_WIDTH ** -0.5),
        'w_br_b': nrm((Dp, GQA_Q_WIDTH, D), BETA * GQA_Q_WIDTH ** -0.5),
        'w_br_c': nrm((Dp, HY_WIDTH, D), BETA * HY_WIDTH ** -0.5),
        'w_out': nrm((Dp, D, D), BETA * D ** -0.5),
        'ln1_g': 1.0 + nrm((Dp, D), 0.02),
        'ln1_b': nrm((Dp, D), 0.01),
        'peer_w_q': nrm((Dp, D, PEER_HEADS * PEER_DKEY), D ** -0.5),
        'peer_sub_keys': nrm((Dp, 2, PEER_NKEYS, PEER_DKEY // 2), (PEER_DKEY // 2) ** -0.5),
        'peer_u': nrm((Dp, PEER_EXPERTS, D), D ** -0.5),
        'peer_v': nrm((Dp, PEER_EXPERTS, D), BETA),
        'ln2_g': 1.0 + nrm((Dp, D), 0.02),
        'ln2_b': nrm((Dp, D), 0.01),
    }


def reference(x, c, ctx, c_ctx, w_ada, b_ada, w_in, na_rpb, gqa_q_gain, gqa_k_gain,
              hy_short_w, hy_short_b, hy_f_w1, hy_f_b1, hy_f_freq1, hy_f_w2, hy_f_b2, hy_f_freq2,
              hy_f_w3, hy_f_b3, hy_bias, w_br_a, w_br_b, w_br_c, w_out, ln1_g, ln1_b,
              peer_w_q, peer_sub_keys, peer_u, peer_v, ln2_g, ln2_b):
    for l in range(DEPTH):
        need_ctx = l < DEPTH - 1
        mod = jax.nn.silu(c) @ w_ada[l] + b_ada[l]
        mod_c = jax.nn.silu(c_ctx) @ w_ada[l] + b_ada[l]
        sh1, sc1, g1, sh2, sc2, g2 = jnp.split(mod[:, None, :], 6, axis=-1)
        csh1, csc1, cg1, csh2, csc2, cg2 = jnp.split(mod_c[None, None, :], 6, axis=-1)
        h = x * (1.0 + sc1) + sh1
        hc = ctx * (1.0 + csc1) + csh1
        y, yc = mixer_sublayer(h, hc, need_ctx, w_in[l], na_rpb[l], gqa_q_gain[l], gqa_k_gain[l],
                               hy_short_w[l], hy_short_b[l], hy_f_w1[l], hy_f_b1[l], hy_f_freq1[l],
                               hy_f_w2[l], hy_f_b2[l], hy_f_freq2[l], hy_f_w3[l], hy_f_b3[l], hy_bias[l],
                               w_br_a[l], w_br_b[l], w_br_c[l], w_out[l])
        x = layer_norm(ALPHA * x + g1 * y, ln1_g[l], ln1_b[l])
        h = x * (1.0 + sc2) + sh2
        x = layer_norm(ALPHA * x + g2 * peer(h, peer_w_q[l], peer_sub_keys[l], peer_u[l], peer_v[l]), ln2_g[l], ln2_b[l])
        if need_ctx:
            ctx = layer_norm(ALPHA * ctx + cg1 * yc, ln1_g[l], ln1_b[l])
            hc = ctx * (1.0 + csc2) + csh2
            ctx = layer_norm(ALPHA * ctx + cg2 * peer(hc, peer_w_q[l], peer_sub_keys[l], peer_u[l], peer_v[l]), ln2_g[l], ln2_b[l])
    return x
```

```python
import functools
import math

import numpy as np
import jax
import jax.numpy as jnp
from jax import lax
from jax.experimental import pallas as pl
from jax.experimental.pallas import tpu as pltpu

F32 = jnp.float32
BF16 = jnp.bfloat16

GRID_W = 64
NA_HEADS = 8
NA_WIN_ROWS = 8
NA_WIN_COLS = 16
GQA_HEADS = 8
GQA_KV_HEADS = 2
HEAD_DIM = 64
ROPE_THETA = 10000.0
HY_WIDTH = 512
HY_SHORT = 3
HY_FILTER_BANDS = 8
HY_FAST_DECAY = 0.3
HY_SLOW_DECAY = 1.5
HY_DECAY_TARGET = 1e-2
PEER_HEADS = 8
PEER_NKEYS = 128
PEER_TOPK = 16
DEPTH = 4
ALPHA = (2 * DEPTH) ** 0.25
LN_EPS = 1e-5
RMS_EPS = 1e-6

LANES = 128
VMEM_LIMIT = 56 * 1024 * 1024
P_PITCH_PAD = 8
NEG = -1e30


def _cp(sem, vmem=VMEM_LIMIT):
    return pltpu.CompilerParams(dimension_semantics=sem, vmem_limit_bytes=vmem)


def _dot(a, b):
    return jnp.dot(a, b, preferred_element_type=F32)


def _dot_nt(a, b):
    return lax.dot_general(a, b, (((1,), (1,)), ((), ())), preferred_element_type=F32)


def _dot_split(a, b):
    a0 = a.astype(BF16)
    r1 = a - a0.astype(F32)
    a1 = r1.astype(BF16)
    a2 = (r1 - a1.astype(F32)).astype(BF16)
    return _dot(a0, b) + _dot(a1, b) + _dot(a2, b)


def _ada_kernel(c_ref, w_ref, b_ref, o_ref):
    c = c_ref[...]
    s = c * jax.nn.sigmoid(c)
    o_ref[...] = jnp.dot(s, w_ref[...], preferred_element_type=F32,
                         precision=lax.Precision.HIGHEST) + b_ref[...]


def ada_mod(c_all, w, b):
    R, D = c_all.shape
    N = w.shape[1]
    tn = 1024 if N % 1024 == 0 else N
    return pl.pallas_call(
        _ada_kernel, grid=(N // tn,),
        in_specs=[pl.BlockSpec((R, D), lambda j: (0, 0)),
                  pl.BlockSpec((D, tn), lambda j: (0, j)),
                  pl.BlockSpec((1, tn), lambda j: (0, j))],
        out_specs=pl.BlockSpec((R, tn), lambda j: (0, j)),
        out_shape=jax.ShapeDtypeStruct((R, N), F32),
        compiler_params=_cp(("arbitrary",)))(c_all, w, b)


def _modmm_kernel(x_ref, sc_ref, sh_ref, w_ref, o_ref):
    h = x_ref[0] * (1.0 + sc_ref[0]) + sh_ref[0]
    o_ref[0] = _dot(h.astype(BF16), w_ref[...]).astype(o_ref.dtype)


def _mod_index(Bm, B):
    if Bm == B:
        return lambda b, t: (b, 0, 0)
    return lambda b, t: (0, 0, 0)


def mod_matmul(x, sc, sh, w, out_dtype, tm):
    B, T, D = x.shape
    N = w.shape[1]
    mi = _mod_index(sc.shape[0], B)
    return pl.pallas_call(
        _modmm_kernel, grid=(B, T // tm),
        in_specs=[pl.BlockSpec((1, tm, D), lambda b, t: (b, t, 0)),
                  pl.BlockSpec((1, 1, D), mi), pl.BlockSpec((1, 1, D), mi),
                  pl.BlockSpec((D, N), lambda b, t: (0, 0))],
        out_specs=pl.BlockSpec((1, tm, N), lambda b, t: (b, t, 0)),
        out_shape=jax.ShapeDtypeStruct((B, T, N), out_dtype),
        compiler_params=_cp(("parallel", "parallel")))(x, sc, sh, w)


def _gqa_proj_kernel(x_ref, sc_ref, sh_ref, w_ref, gq_ref, gk_ref, gmq_ref, gmk_ref, pq_ref, pk_ref,
                     cos_ref, sin_ref, q_ref, k_ref, v_ref, *, rope, qw):
    h = x_ref[0] * (1.0 + sc_ref[0]) + sh_ref[0]
    z = _dot(h.astype(BF16), w_ref[...])
    q = z[:, :qw]
    k = z[:, qw:qw + LANES]
    v = z[:, qw + LANES:]
    qn = q * lax.rsqrt(_dot_split(q * q, gmq_ref[...]) + RMS_EPS) * gq_ref[...]
    kn = k * lax.rsqrt(_dot_split(k * k, gmk_ref[...]) + RMS_EPS) * gk_ref[...]
    if rope:
        cos = cos_ref[...]
        sin = sin_ref[...]
        cq = jnp.concatenate([cos] * (qw // LANES), axis=1)
        sq = jnp.concatenate([sin] * (qw // LANES), axis=1)
        qn = qn * cq + _dot_split(qn, pq_ref[...]) * sq
        kn = kn * cos + _dot_split(kn, pk_ref[...]) * sin
    q_ref[0] = qn.astype(q_ref.dtype)
    k_ref[0] = kn.astype(k_ref.dtype)
    v_ref[0] = v.astype(v_ref.dtype)


def gqa_proj(x, sc, sh, w, gq, gk, gmq, gmk, pq, pk, cos, sin, rope, tm):
    B, T, D = x.shape
    N = w.shape[1]
    qw = N - 2 * LANES
    mi = _mod_index(sc.shape[0], B)
    c2 = lambda b, t: (0, 0)
    kern = functools.partial(_gqa_proj_kernel, rope=rope, qw=qw)
    return pl.pallas_call(
        kern, grid=(B, T // tm),
        in_specs=[pl.BlockSpec((1, tm, D), lambda b, t: (b, t, 0)),
                  pl.BlockSpec((1, 1, D), mi), pl.BlockSpec((1, 1, D), mi),
                  pl.BlockSpec((D, N), c2),
                  pl.BlockSpec((1, qw), c2), pl.BlockSpec((1, LANES), c2),
                  pl.BlockSpec((qw, qw), c2), pl.BlockSpec((LANES, LANES), c2),
                  pl.BlockSpec((qw, qw), c2), pl.BlockSpec((LANES, LANES), c2),
                  pl.BlockSpec((tm, LANES), lambda b, t: (t, 0)),
                  pl.BlockSpec((tm, LANES), lambda b, t: (t, 0))],
        out_specs=[pl.BlockSpec((1, tm, qw), lambda b, t: (b, t, 0)),
                   pl.BlockSpec((1, tm, LANES), lambda b, t: (b, t, 0)),
                   pl.BlockSpec((1, tm, LANES), lambda b, t: (b, t, 0))],
        out_shape=[jax.ShapeDtypeStruct((B, T, qw), BF16),
                   jax.ShapeDtypeStruct((B, T, LANES), BF16),
                   jax.ShapeDtypeStruct((B, T, LANES), BF16)],
        compiler_params=_cp(("parallel", "parallel")))(x, sc, sh, w, gq, gk, gmq, gmk, pq, pk, cos, sin)


def _gqa_attn_kernel(q_ref, k_ref, v_ref, o_ref, *, nheads, scale):
    k = k_ref[0]
    v = v_ref[0]
    for h in range(nheads):
        sl = slice(h * LANES, (h + 1) * LANES)
        s = _dot_nt(q_ref[0, :, sl], k) * scale
        m = jnp.max(s, axis=-1, keepdims=True)
        p = jnp.exp(s - m)
        den = jnp.sum(p, axis=-1, keepdims=True)
        o = _dot(p.astype(BF16), v)
        o_ref[0, :, sl] = (o / den).astype(o_ref.dtype)


def gqa_attention(q, k, v, tq):
    B, T, QW = q.shape
    Lk = k.shape[1]
    kern = functools.partial(_gqa_attn_kernel, nheads=QW // LANES, scale=HEAD_DIM ** -0.5)
    return pl.pallas_call(
        kern, grid=(B, T // tq),
        in_specs=[pl.BlockSpec((1, tq, QW), lambda b, t: (b, t, 0)),
                  pl.BlockSpec((1, Lk, LANES), lambda b, t: (b, 0, 0)),
                  pl.BlockSpec((1, Lk, LANES), lambda b, t: (b, 0, 0))],
        out_specs=pl.BlockSpec((1, tq, QW), lambda b, t: (b, t, 0)),
        out_shape=jax.ShapeDtypeStruct((B, T, QW), BF16),
        compiler_params=_cp(("parallel", "parallel")))(q, k, v)


def _pair_attention(q2, k_loc, v_loc, bias_pair, k_ctx, v_ctx, scale):
    lane = lax.broadcasted_iota(jnp.int32, q2.shape, 1)
    outs = []
    for half in range(2):
        msk = (lane >= HEAD_DIM) if half else (lane < HEAD_DIM)
        qm = jnp.where(msk, q2, jnp.zeros_like(q2))
        s_ctx = _dot_nt(qm, k_ctx) * scale
        mx = jnp.max(s_ctx, axis=-1, keepdims=True)
        if k_loc is not None:
            s_loc = _dot_nt(qm, k_loc) * scale + bias_pair[half]
            mx = jnp.maximum(mx, jnp.max(s_loc, axis=-1, keepdims=True))
            p_loc = jnp.exp(s_loc - mx)
        p_ctx = jnp.exp(s_ctx - mx)
        den = jnp.sum(p_ctx, axis=-1, keepdims=True)
        o = _dot(p_ctx.astype(BF16), v_ctx)
        if k_loc is not None:
            den = den + jnp.sum(p_loc, axis=-1, keepdims=True)
            o = o + _dot(p_loc.astype(BF16), v_loc)
        outs.append(o / den)
    return jnp.where(lane < HEAD_DIM, outs[0], outs[1])


def _na_kernel(q_ref, k_ref, v_ref, kc_ref, vc_ref, bt_ref, o_ref, *, rows, wr, width, scale):
    r = pl.program_id(1)
    r0 = jnp.clip(r - wr // 2, 0, rows - wr)
    start = pl.multiple_of(r0 * width, width)
    nk = wr * width
    for pr in range(q_ref.shape[2] // LANES):
        sl = slice(pr * LANES, (pr + 1) * LANES)
        o = _pair_attention(q_ref[0, :, sl],
                            k_ref[0, pl.ds(start, nk), sl], v_ref[0, pl.ds(start, nk), sl],
                            (bt_ref[0, 2 * pr], bt_ref[0, 2 * pr + 1]),
                            kc_ref[0, :, sl], vc_ref[0, :, sl], scale)
        o_ref[0, :, sl] = o.astype(o_ref.dtype)


def na_attention(qkv, qkv_c, bias_tab, width, wr):
    B, T, W3 = qkv.shape
    HW = W3 // 3
    C = qkv_c.shape[1]
    rows = T // width
    kern = functools.partial(_na_kernel, rows=rows, wr=wr, width=width, scale=HEAD_DIM ** -0.5)

    def bt_index(b, r):
        r0 = jnp.clip(r - wr // 2, 0, rows - wr)
        return (r0 - r + wr - 1, 0, 0, 0)

    return pl.pallas_call(
        kern, grid=(B, rows),
        in_specs=[pl.BlockSpec((1, width, HW), lambda b, r: (b, r, 0)),
                  pl.BlockSpec((1, T, HW), lambda b, r: (b, 0, 1)),
                  pl.BlockSpec((1, T, HW), lambda b, r: (b, 0, 2)),
                  pl.BlockSpec((1, C, HW), lambda b, r: (b, 0, 1)),
                  pl.BlockSpec((1, C, HW), lambda b, r: (b, 0, 2)),
                  pl.BlockSpec((1,) + bias_tab.shape[1:], bt_index)],
        out_specs=pl.BlockSpec((1, width, HW), lambda b, r: (b, r, 0)),
        out_shape=jax.ShapeDtypeStruct((B, T, HW), BF16),
        compiler_params=_cp(("parallel", "arbitrary")))(qkv, qkv, qkv, qkv_c, qkv_c, bias_tab)


def _ctx_pair_kernel(q_ref, k_ref, v_ref, o_ref, *, scale):
    for pr in range(q_ref.shape[2] // LANES):
        sl = slice(pr * LANES, (pr + 1) * LANES)
        o = _pair_attention(q_ref[0, :, sl], None, None, None, k_ref[0, :, sl], v_ref[0, :, sl], scale)
        o_ref[0, :, sl] = o.astype(o_ref.dtype)


def ctx_pair_attention(qkv_c):
    B, C, W3 = qkv_c.shape
    HW = W3 // 3
    kern = functools.partial(_ctx_pair_kernel, scale=HEAD_DIM ** -0.5)
    return pl.pallas_call(
        kern, grid=(B,),
        in_specs=[pl.BlockSpec((1, C, HW), lambda b: (b, 0, 0)),
                  pl.BlockSpec((1, C, HW), lambda b: (b, 0, 1)),
                  pl.BlockSpec((1, C, HW), lambda b: (b, 0, 2))],
        out_specs=pl.BlockSpec((1, C, HW), lambda b: (b, 0, 0)),
        out_shape=jax.ShapeDtypeStruct((B, C, HW), BF16),
        compiler_params=_cp(("parallel",)))(qkv_c, qkv_c, qkv_c)


def _layer_norm(z, g, b):
    mu = jnp.mean(z, axis=-1, keepdims=True)
    zc = z - mu
    var = jnp.mean(zc * zc, axis=-1, keepdims=True)
    return zc * lax.rsqrt(var + LN_EPS) * g + b


def _merge_kernel(x_ref, sc_ref, sh_ref, g1_ref, ya_ref, yb_ref, yc_ref, wg_ref, wa_ref, wb_ref, wc_ref,
                  wo_ref, lng_ref, lnb_ref, o_ref):
    x = x_ref[0]
    D = x.shape[1]
    h = x * (1.0 + sc_ref[0]) + sh_ref[0]
    gl = jax.nn.sigmoid(_dot(h.astype(BF16), wg_ref[...]))
    m = (gl[:, :D] * _dot(ya_ref[0], wa_ref[...])
         + gl[:, D:2 * D] * _dot(yb_ref[0], wb_ref[...])
         + gl[:, 2 * D:] * _dot(yc_ref[0].astype(BF16), wc_ref[...]))
    y = _dot(m.astype(BF16), wo_ref[...])
    o_ref[0] = _layer_norm(ALPHA * x + g1_ref[0] * y, lng_ref[...], lnb_ref[...])


def merge_block(x, sc, sh, g1, ya, yb, yc, wg, wa, wb, wc, wo, lng, lnb, tm):
    B, T, D = x.shape
    mi = _mod_index(sc.shape[0], B)
    c2 = lambda b, t: (0, 0)
    row = lambda w: pl.BlockSpec((1, tm, w), lambda b, t: (b, t, 0))
    full = lambda a: pl.BlockSpec(a.shape, c2)
    return pl.pallas_call(
        _merge_kernel, grid=(B, T // tm),
        in_specs=[row(D), pl.BlockSpec((1, 1, D), mi), pl.BlockSpec((1, 1, D), mi), pl.BlockSpec((1, 1, D), mi),
                  row(ya.shape[2]), row(yb.shape[2]), row(yc.shape[2]),
                  full(wg), full(wa), full(wb), full(wc), full(wo), full(lng), full(lnb)],
        out_specs=row(D),
        out_shape=jax.ShapeDtypeStruct((B, T, D), F32),
        compiler_params=_cp(("parallel", "parallel")))(x, sc, sh, g1, ya, yb, yc, wg, wa, wb, wc, wo, lng, lnb)


def _topk_rows(s, k):
    R = s.shape[0]
    iota = lax.broadcasted_iota(jnp.int32, s.shape, 0)
    vals, idxs = [], []
    for _ in range(k):
        m = jnp.max(s, axis=0, keepdims=True)
        idx = jnp.min(jnp.where(s == m, iota, R), axis=0, keepdims=True)
        vals.append(m)
        idxs.append(idx)
        s = jnp.where(iota == idx, -jnp.inf, s)
    return jnp.concatenate(vals, axis=0), jnp.concatenate(idxs, axis=0)


def _route_kernel(x_ref, sc_ref, sh_ref, wq_ref, sk_ref, h_ref, i_ref, j_ref, g_ref, *, nheads, nkeys, topk):
    h = (x_ref[0] * (1.0 + sc_ref[0]) + sh_ref[0]).astype(BF16)
    h_ref[0] = h
    q = _dot(h, wq_ref[...])
    tm = q.shape[0]
    dk = sk_ref.shape[2]
    i_all, j_all, g_all = [], [], []
    for hd in range(nheads):
        tops = []
        for p in range(2):
            qs = q[:, (2 * hd + p) * dk:(2 * hd + p + 1) * dk].astype(BF16)
            st = _dot_nt(sk_ref[p], qs)
            tops.append(_topk_rows(st, topk))
        (v0, i0), (v1, i1) = tops
        cand = (v0[:, None, :] + v1[None, :, :]).reshape(topk * topk, tm)
        cidx = (i0[:, None, :] * nkeys + i1[None, :, :]).reshape(topk * topk, tm)
        iota = lax.broadcasted_iota(jnp.int32, cand.shape, 0)
        best, eidx = [], []
        for _ in range(topk):
            m = jnp.max(cand, axis=0, keepdims=True)
            pos = jnp.min(jnp.where(cand == m, iota, topk * topk), axis=0, keepdims=True)
            hit = iota == pos
            best.append(m)
            eidx.append(jnp.sum(jnp.where(hit, cidx, 0), axis=0, keepdims=True))
            cand = jnp.where(hit, -jnp.inf, cand)
        best = jnp.concatenate(best, axis=0)
        e = jnp.concatenate(eidx, axis=0)
        pe = jnp.exp(best - best[0:1])
        g_all.append(pe / jnp.sum(pe, axis=0, keepdims=True))
        i_all.append(e >> (nkeys.bit_length() - 1))
        j_all.append(e & (nkeys - 1))
    i_ref[0] = jnp.concatenate(i_all, axis=0).T
    j_ref[0] = jnp.concatenate(j_all, axis=0).T
    g_ref[0] = jnp.concatenate(g_all, axis=0).T


def peer_route(x, sc, sh, wq, sk, tm):
    B, T, D = x.shape
    mi = _mod_index(sc.shape[0], B)
    npairs = PEER_HEADS * PEER_TOPK
    kern = functools.partial(_route_kernel, nheads=PEER_HEADS, nkeys=sk.shape[1], topk=PEER_TOPK)
    row = lambda w: pl.BlockSpec((1, tm, w), lambda b, t: (b, t, 0))
    return pl.pallas_call(
        kern, grid=(B, T // tm),
        in_specs=[row(D), pl.BlockSpec((1, 1, D), mi), pl.BlockSpec((1, 1, D), mi),
                  pl.BlockSpec(wq.shape, lambda b, t: (0, 0)),
                  pl.BlockSpec(sk.shape, lambda b, t: (0, 0, 0))],
        out_specs=[row(D), row(npairs), row(npairs), row(npairs)],
        out_shape=[jax.ShapeDtypeStruct((B, T, D), BF16),
                   jax.ShapeDtypeStruct((B, T, npairs), jnp.int32),
                   jax.ShapeDtypeStruct((B, T, npairs), jnp.int32),
                   jax.ShapeDtypeStruct((B, T, npairs), F32)],
        compiler_params=_cp(("parallel", "parallel")))(x, sc, sh, wq, sk)


def _peer_act_kernel(h_ref, i_ref, j_ref, g_ref, u_ref, w_ref, a_ref, *, ib, nkeys):
    s_id = pl.program_id(2)

    @pl.when(s_id == 0)
    def _():
        a_ref[...] = jnp.zeros_like(a_ref)

    sc = _dot_nt(h_ref[0], u_ref[...])
    irow = i_ref[0]
    jcol = j_ref[0]
    a = a_ref[...]
    for ii in range(ib):
        got = jnp.take_along_axis(sc[:, ii * nkeys:(ii + 1) * nkeys], jcol, axis=1, mode="promise_in_bounds")
        a = jnp.where(irow == s_id * ib + ii, got, a)
    a_ref[...] = a

    @pl.when(s_id == pl.num_programs(2) - 1)
    def _():
        act = 0.5 * a * (1.0 + lax.erf(a * (2.0 ** -0.5)))
        w_ref[0] = act * g_ref[0]


def peer_act(hb, ii, jj, g, u, tm, ib):
    B, T, D = hb.shape
    npairs = ii.shape[2]
    nkeys = PEER_NKEYS
    steps = u.shape[0] // (ib * nkeys)
    kern = functools.partial(_peer_act_kernel, ib=ib, nkeys=nkeys)
    row = lambda w: pl.BlockSpec((1, tm, w), lambda b, t, s: (b, t, 0))
    return pl.pallas_call(
        kern, grid=(B, T // tm, steps),
        in_specs=[row(D), row(npairs), row(npairs), row(npairs),
                  pl.BlockSpec((ib * nkeys, D), lambda b, t, s: (s, 0))],
        out_specs=row(npairs),
        out_shape=jax.ShapeDtypeStruct((B, T, npairs), F32),
        scratch_shapes=[pltpu.VMEM((tm, npairs), F32)],
        compiler_params=_cp(("parallel", "parallel", "arbitrary")))(hb, ii, jj, g, u)


def _peer_mix_kernel(x_ref, g2_ref, i_ref, j_ref, w_ref, v_ref, lng_ref, lnb_ref, o_ref, p_ref, acc_ref,
                     *, ib, nkeys, pitch):
    s_id = pl.program_id(2)
    tm = x_ref.shape[1]

    @pl.when(s_id == 0)
    def _():
        acc_ref[...] = jnp.zeros_like(acc_ref)
        sub = lax.broadcasted_iota(jnp.int32, (nkeys, i_ref.shape[2]), 0)

        def body(n, carry):
            irow = i_ref[0, pl.ds(n, 1), :]
            jrow = j_ref[0, pl.ds(n, 1), :]
            wrow = w_ref[0, pl.ds(n, 1), :]
            lhs = jnp.where(sub == irow, wrow, 0.0).astype(BF16)
            rhs = jnp.where(sub == jrow, 1.0, 0.0).astype(BF16)
            p_ref[pl.ds(n, nkeys, stride=pitch), :] = _dot_nt(lhs, rhs)
            return carry

        lax.fori_loop(0, tm, body, 0)

    parts = [p_ref[pl.ds(pl.multiple_of((s_id * ib + ii) * pitch, 8), tm), :].astype(BF16) for ii in range(ib)]
    acc_ref[...] += _dot(jnp.concatenate(parts, axis=1), v_ref[...])

    @pl.when(s_id == pl.num_programs(2) - 1)
    def _():
        o_ref[0] = _layer_norm(ALPHA * x_ref[0] + g2_ref[0] * acc_ref[...], lng_ref[...], lnb_ref[...])


def peer_mix(x, g2, ii, jj, w, v, lng, lnb, tm, ib):
    B, T, D = x.shape
    npairs = ii.shape[2]
    nkeys = PEER_NKEYS
    steps = v.shape[0] // (ib * nkeys)
    pitch = tm + P_PITCH_PAD
    mi = _mod_index(g2.shape[0], B)
    kern = functools.partial(_peer_mix_kernel, ib=ib, nkeys=nkeys, pitch=pitch)
    row = lambda w_: pl.BlockSpec((1, tm, w_), lambda b, t, s: (b, t, 0))
    c2 = lambda b, t, s: (0, 0)
    return pl.pallas_call(
        kern, grid=(B, T // tm, steps),
        in_specs=[row(D), pl.BlockSpec((1, 1, D), lambda b, t, s: mi(b, t)),
                  row(npairs), row(npairs), row(npairs),
                  pl.BlockSpec((ib * nkeys, D), lambda b, t, s: (s, 0)),
                  pl.BlockSpec(lng.shape, c2), pl.BlockSpec(lnb.shape, c2)],
        out_specs=row(D),
        out_shape=jax.ShapeDtypeStruct((B, T, D), F32),
        scratch_shapes=[pltpu.VMEM((nkeys * pitch, nkeys), F32), pltpu.VMEM((tm, D), F32)],
        compiler_params=_cp(("parallel", "parallel", "arbitrary")))(x, g2, ii, jj, w, v, lng, lnb)


def _short_conv(x, w, b):
    ch = x.shape[-1]
    y = lax.conv_general_dilated(x, w[:, None, :].astype(x.dtype), window_strides=(1,),
                                 padding=[(HY_SHORT // 2, HY_SHORT // 2)],
                                 dimension_numbers=('NWC', 'WIO', 'NWC'), feature_group_count=ch)
    return y + b


def _hyena_filter(L, w1, b1, fr1, w2, b2, fr2, w3, b3):
    t = jnp.linspace(0.0, 1.0, L, dtype=F32)
    w = 2.0 * math.pi * jnp.arange(L, dtype=F32) / L
    bands = jnp.linspace(1e-4, HY_FILTER_BANDS - 1, HY_FILTER_BANDS, dtype=F32)
    z = jnp.concatenate([t[:, None], jnp.cos(w[:, None] * bands[None]), -jnp.sin(w[:, None] * bands[None])], axis=-1)
    hp = lax.Precision.HIGHEST
    hdn = jnp.sin(fr1 * (jnp.dot(z, w1, precision=hp) + b1))
    hdn = jnp.sin(fr2 * (jnp.dot(hdn, w2, precision=hp) + b2))
    k = (jnp.dot(hdn, w3, precision=hp) + b3).astype(F32).reshape(L, 2, HY_WIDTH)
    max_decay = math.log(HY_DECAY_TARGET) / HY_FAST_DECAY
    min_decay = math.log(HY_DECAY_TARGET) / HY_SLOW_DECAY
    deltas = jnp.linspace(min_decay, max_decay, HY_WIDTH, dtype=F32)
    decay = jnp.exp(-t[:, None] * jnp.abs(deltas)[None, :])
    k = k * decay[:, None, :]
    kfull = jnp.concatenate([k[:, 0], jnp.zeros((1, HY_WIDTH), F32), k[:0:-1, 1]], axis=0)
    return kfull / jnp.sum(jnp.abs(kfull), axis=0, keepdims=True)


def _bidir_fftconv(u, kfull, bias):
    L = u.shape[1]
    uf = jnp.fft.rfft(u.astype(F32), n=2 * L, axis=1)
    kf = jnp.fft.rfft(kfull, n=2 * L, axis=0)
    y = jnp.fft.irfft(uf * kf[None], n=2 * L, axis=1)[:, :L]
    return y + u * bias


def hyena_mixer(zh, short_w, short_b, kfull, hy_bias):
    u = _short_conv(zh, short_w, short_b)
    x0, x1, v = jnp.split(u, 3, axis=-1)
    return _bidir_fftconv(v * x1, kfull, hy_bias) * x0


def _gqa_slot_offsets():
    group = GQA_HEADS // GQA_KV_HEADS
    return [(h * LANES + (h // group) * HEAD_DIM) for h in range(GQA_HEADS)]


def _pad_gqa_cols(w):
    out = jnp.zeros((w.shape[0], GQA_HEADS * LANES), w.dtype)
    for h, off in enumerate(_gqa_slot_offsets()):
        out = out.at[:, off:off + HEAD_DIM].set(w[:, h * HEAD_DIM:(h + 1) * HEAD_DIM])
    return out


def _pad_gqa_vec(g):
    out = jnp.zeros((GQA_HEADS * LANES,), g.dtype)
    for off in _gqa_slot_offsets():
        out = out.at[off:off + HEAD_DIM].set(g)
    return out


def _rope_consts(T):
    half = HEAD_DIM // 2
    quarter = half // 2
    t = jnp.arange(T)
    inv = ROPE_THETA ** (-jnp.arange(0, half, 2, dtype=F32) / half)
    ar = (t // GRID_W).astype(F32)[:, None] * inv[None, :]
    ac = (t % GRID_W).astype(F32)[:, None] * inv[None, :]
    ang = jnp.concatenate([ar, ar, ac, ac] * (LANES // HEAD_DIM), axis=1)
    def perm(width, valid):
        P = np.zeros((width, width), np.float32)
        for o in range(0, width, half):
            if not valid(o):
                continue
            for j in range(quarter):
                P[o + j + quarter, o + j] = -1.0
                P[o + j, o + j + quarter] = 1.0
        return jnp.asarray(P, BF16)
    offs = _gqa_slot_offsets()
    qvalid = lambda o: any(off <= o < off + HEAD_DIM for off in offs)
    return jnp.cos(ang), jnp.sin(ang), perm(GQA_HEADS * LANES, qvalid), perm(LANES, lambda o: True)


def _group_mean_mats():
    qw = GQA_HEADS * LANES
    gq = np.zeros((qw, qw), np.float32)
    for s in range(0, qw, LANES):
        gq[s:s + LANES, s:s + LANES] = 1.0 / HEAD_DIM
    gk = np.zeros((LANES, LANES), np.float32)
    for s in range(0, LANES, HEAD_DIM):
        gk[s:s + HEAD_DIM, s:s + HEAD_DIM] = 1.0 / HEAD_DIM
    return jnp.asarray(gq, BF16), jnp.asarray(gk, BF16)


def _na_bias_table(rpb, rows):
    wr = min(NA_WIN_ROWS, rows)
    wc = NA_WIN_COLS
    q = np.arange(GRID_W)
    c0 = np.clip(q - wc // 2, 0, GRID_W - wc)
    kc = np.arange(GRID_W)
    valid = (kc[None, :] >= c0[:, None]) & (kc[None, :] < c0[:, None] + wc)
    dc = np.clip(kc[None, :] - q[:, None] + (NA_WIN_COLS - 1), 0, 2 * NA_WIN_COLS - 2)
    o = np.arange(wr)
    p = np.arange(wr)
    dr = np.clip(o[:, None] - (wr - 1) + p[None, :] + (NA_WIN_ROWS - 1), 0, 2 * NA_WIN_ROWS - 2)
    tab = rpb[:, dr[:, :, None, None], dc[None, None, :, :]]
    tab = jnp.where(jnp.asarray(valid)[None, None, None], tab, NEG)
    tab = jnp.transpose(tab, (1, 0, 3, 2, 4))
    return tab.reshape(wr, rpb.shape[0], GRID_W, wr * GRID_W)


def kernel(x, c, ctx, c_ctx, w_ada, b_ada, w_in, na_rpb, gqa_q_gain, gqa_k_gain, hy_short_w, hy_short_b,
           hy_f_w1, hy_f_b1, hy_f_freq1, hy_f_w2, hy_f_b2, hy_f_freq2, hy_f_w3, hy_f_b3, hy_bias,
           w_br_a, w_br_b, w_br_c, w_out, ln1_g, ln1_b, peer_w_q, peer_sub_keys, peer_u, peer_v, ln2_g, ln2_b):
    B, T, D = x.shape
    C = ctx.shape[1]
    depth = w_ada.shape[0]
    rows = T // GRID_W
    wr = min(NA_WIN_ROWS, rows)
    NAW = NA_HEADS * HEAD_DIM
    GQW = GQA_HEADS * HEAD_DIM
    GKW = GQA_KV_HEADS * HEAD_DIM
    o_na, o_gq, o_gk, o_gv, o_hy, o_gl = 0, 3 * NAW, 3 * NAW + GQW, 3 * NAW + GQW + GKW, 3 * NAW + GQW + 2 * GKW, \
        3 * NAW + GQW + 2 * GKW + 3 * HY_WIDTH

    cos_t, sin_t, perm_q, perm_k = _rope_consts(T)
    gm_q, gm_k = _group_mean_mats()
    c_all = jnp.concatenate([c, c_ctx[None, :], jnp.zeros((16 - B - 1, D), F32)], axis=0)
    tm = min(512, T)
    tmc = C

    for l in range(depth):
        need_ctx = l < depth - 1
        mod = ada_mod(c_all, w_ada[l], b_ada[l][None, :])
        ml = mod[:B].reshape(B, 1, 6, D)
        mc = mod[B:B + 1].reshape(1, 1, 6, D)
        sh1, sc1, g1, sh2, sc2, g2 = [ml[:, :, i] for i in range(6)]
        csh1, csc1, cg1, csh2, csc2, cg2 = [mc[:, :, i] for i in range(6)]

        wl = w_in[l]
        w_na = wl[:, o_na:o_gq].astype(BF16)
        w_gq = jnp.concatenate([_pad_gqa_cols(wl[:, o_gq:o_gk]), wl[:, o_gk:o_hy]], axis=1).astype(BF16)
        w_hy = wl[:, o_hy:o_gl].astype(BF16)
        w_gl = wl[:, o_gl:].astype(BF16)
        gq = _pad_gqa_vec(gqa_q_gain[l])[None, :]
        gk = jnp.tile(gqa_k_gain[l], GQA_KV_HEADS)[None, :]
        wa = w_br_a[l].astype(BF16)
        wb = _pad_gqa_cols(w_br_b[l].T).T.astype(BF16)
        wc = w_br_c[l].astype(BF16)
        wo = w_out[l].astype(BF16)
        lng1, lnb1 = ln1_g[l][None, :], ln1_b[l][None, :]
        lng2, lnb2 = ln2_g[l][None, :], ln2_b[l][None, :]
        wq = peer_w_q[l].astype(BF16)
        sk = peer_sub_keys[l].astype(BF16)
        ub = peer_u[l].astype(BF16)
        vb = peer_v[l].astype(BF16)
        bias_tab = _na_bias_table(na_rpb[l], rows)

        qkv_a = mod_matmul(x, sc1, sh1, w_na, BF16, tm)
        qkv_ac = mod_matmul(ctx, csc1, csh1, w_na, BF16, tmc)
        ya = na_attention(qkv_a, qkv_ac, bias_tab, GRID_W, wr)

        qb, kb, vb_ = gqa_proj(x, sc1, sh1, w_gq, gq, gk, gm_q, gm_k, perm_q, perm_k, cos_t, sin_t, True, tm)
        qbc, kbc, vbc = gqa_proj(ctx, csc1, csh1, w_gq, gq, gk, gm_q, gm_k, perm_q, perm_k,
                                 cos_t[:C], sin_t[:C], False, tmc)
        yb = gqa_attention(qb, jnp.concatenate([kb, kbc], axis=1), jnp.concatenate([vb_, vbc], axis=1), min(256, T))

        zh = mod_matmul(x, sc1, sh1, w_hy, F32, tm)
        filt = functools.partial(_hyena_filter, w1=hy_f_w1[l], b1=hy_f_b1[l], fr1=hy_f_freq1[l], w2=hy_f_w2[l],
                                 b2=hy_f_b2[l], fr2=hy_f_freq2[l], w3=hy_f_w3[l], b3=hy_f_b3[l])
        yc = hyena_mixer(zh, hy_short_w[l], hy_short_b[l], filt(T), hy_bias[l])

        x_mid = merge_block(x, sc1, sh1, g1, ya, yb, yc, w_gl, wa, wb, wc, wo, lng1, lnb1, min(256, T))

        hb, ii, jj, gg = peer_route(x_mid, sc2, sh2, wq, sk, min(256, T))
        wgt = peer_act(hb, ii, jj, gg, ub, min(1024, T), 8)
        x_new = peer_mix(x_mid, g2, ii, jj, wgt, vb, lng2, lnb2, min(256, T), 16)

        if need_ctx:
            yac = ctx_pair_attention(qkv_ac)
            ybc = gqa_attention(qbc, kbc, vbc, C)
            zhc = mod_matmul(ctx, csc1, csh1, w_hy, F32, tmc)
            ycc = hyena_mixer(zhc, hy_short_w[l], hy_short_b[l], filt(C), hy_bias[l])
            c_mid = merge_block(ctx, csc1, csh1, cg1, yac, ybc, ycc, w_gl, wa, wb, wc, wo, lng1, lnb1, tmc)
            hbc, iic, jjc, ggc = peer_route(c_mid, csc2, csh2, wq, sk, tmc)
            wgtc = peer_act(hbc, iic, jjc, ggc, ub, tmc, 8)
            ctx = peer_mix(c_mid, cg2, iic, jjc, wgtc, vb, lng2, lnb2, tmc, 16)
        x = x_new
    return x
```

```python
import functools
import math

import numpy as np
import jax
import jax.numpy as jnp
from jax import lax
from jax.experimental import pallas as pl
from jax.experimental.pallas import tpu as pltpu

F32 = jnp.float32
BF16 = jnp.bfloat16

GRID_W = 64
NA_HEADS = 8
NA_WIN_ROWS = 8
NA_WIN_COLS = 16
GQA_HEADS = 8
GQA_KV_HEADS = 2
HEAD_DIM = 64
ROPE_THETA = 10000.0
HY_WIDTH = 512
HY_SHORT = 3
HY_FILTER_BANDS = 8
HY_FAST_DECAY = 0.3
HY_SLOW_DECAY = 1.5
HY_DECAY_TARGET = 1e-2
PEER_HEADS = 8
PEER_NKEYS = 128
PEER_TOPK = 16
DEPTH = 4
ALPHA = (2 * DEPTH) ** 0.25
LN_EPS = 1e-5
RMS_EPS = 1e-6

LANES = 128
SUBLANES = 8
HY_N2 = 128
HY_CB = LANES
FFT_PITCH_PAD = 8
MIX_UNROLL = 8
VMEM_LIMIT = 56 * 1024 * 1024
P_PITCH_PAD = 8
NEG = -1e30


def _cp(sem, vmem=VMEM_LIMIT):
    return pltpu.CompilerParams(dimension_semantics=sem, vmem_limit_bytes=vmem)


def _dot(a, b):
    return jnp.dot(a, b, preferred_element_type=F32)


def _dot_nt(a, b):
    return lax.dot_general(a, b, (((1,), (1,)), ((), ())), preferred_element_type=F32)


def _dot_split(a, b):
    a0 = a.astype(BF16)
    r1 = a - a0.astype(F32)
    a1 = r1.astype(BF16)
    a2 = (r1 - a1.astype(F32)).astype(BF16)
    return _dot(a0, b) + _dot(a1, b) + _dot(a2, b)


def _ada_kernel(c_ref, w_ref, b_ref, o_ref):
    c = c_ref[...]
    s = c * jax.nn.sigmoid(c)
    o_ref[...] = jnp.dot(s, w_ref[...], preferred_element_type=F32,
                         precision=lax.Precision.HIGHEST) + b_ref[...]


def ada_mod(c_all, w, b):
    R, D = c_all.shape
    N = w.shape[1]
    tn = 1024 if N % 1024 == 0 else N
    return pl.pallas_call(
        _ada_kernel, grid=(N // tn,),
        in_specs=[pl.BlockSpec((R, D), lambda j: (0, 0)),
                  pl.BlockSpec((D, tn), lambda j: (0, j)),
                  pl.BlockSpec((1, tn), lambda j: (0, j))],
        out_specs=pl.BlockSpec((R, tn), lambda j: (0, j)),
        out_shape=jax.ShapeDtypeStruct((R, N), F32),
        compiler_params=_cp(("arbitrary",)))(c_all, w, b)


def _modmm_kernel(x_ref, sc_ref, sh_ref, w_ref, o_ref):
    h = x_ref[0] * (1.0 + sc_ref[0]) + sh_ref[0]
    o_ref[0] = _dot(h.astype(BF16), w_ref[...]).astype(o_ref.dtype)


def _mod_index(Bm, B):
    if Bm == B:
        return lambda b, t: (b, 0, 0)
    return lambda b, t: (0, 0, 0)


def mod_matmul(x, sc, sh, w, out_dtype, tm):
    B, T, D = x.shape
    N = w.shape[1]
    mi = _mod_index(sc.shape[0], B)
    return pl.pallas_call(
        _modmm_kernel, grid=(B, T // tm),
        in_specs=[pl.BlockSpec((1, tm, D), lambda b, t: (b, t, 0)),
                  pl.BlockSpec((1, 1, D), mi), pl.BlockSpec((1, 1, D), mi),
                  pl.BlockSpec((D, N), lambda b, t: (0, 0))],
        out_specs=pl.BlockSpec((1, tm, N), lambda b, t: (b, t, 0)),
        out_shape=jax.ShapeDtypeStruct((B, T, N), out_dtype),
        compiler_params=_cp(("parallel", "parallel")))(x, sc, sh, w)


def _gqa_proj_kernel(x_ref, sc_ref, sh_ref, w_ref, gq_ref, gk_ref, gmq_ref, gmk_ref, pq_ref, pk_ref,
                     cos_ref, sin_ref, q_ref, k_ref, v_ref, *, rope, qw):
    h = x_ref[0] * (1.0 + sc_ref[0]) + sh_ref[0]
    z = _dot(h.astype(BF16), w_ref[...])
    q = z[:, :qw]
    k = z[:, qw:qw + LANES]
    v = z[:, qw + LANES:]
    qn = q * lax.rsqrt(_dot_split(q * q, gmq_ref[...]) + RMS_EPS) * gq_ref[...]
    kn = k * lax.rsqrt(_dot_split(k * k, gmk_ref[...]) + RMS_EPS) * gk_ref[...]
    if rope:
        cos = cos_ref[...]
        sin = sin_ref[...]
        cq = jnp.concatenate([cos] * (qw // LANES), axis=1)
        sq = jnp.concatenate([sin] * (qw // LANES), axis=1)
        qn = qn * cq + _dot_split(qn, pq_ref[...]) * sq
        kn = kn * cos + _dot_split(kn, pk_ref[...]) * sin
    q_ref[0] = qn.astype(q_ref.dtype)
    k_ref[0] = kn.astype(k_ref.dtype)
    v_ref[0] = v.astype(v_ref.dtype)


def gqa_proj(x, sc, sh, w, gq, gk, gmq, gmk, pq, pk, cos, sin, rope, tm):
    B, T, D = x.shape
    N = w.shape[1]
    qw = N - 2 * LANES
    mi = _mod_index(sc.shape[0], B)
    c2 = lambda b, t: (0, 0)
    kern = functools.partial(_gqa_proj_kernel, rope=rope, qw=qw)
    return pl.pallas_call(
        kern, grid=(B, T // tm),
        in_specs=[pl.BlockSpec((1, tm, D), lambda b, t: (b, t, 0)),
                  pl.BlockSpec((1, 1, D), mi), pl.BlockSpec((1, 1, D), mi),
                  pl.BlockSpec((D, N), c2),
                  pl.BlockSpec((1, qw), c2), pl.BlockSpec((1, LANES), c2),
                  pl.BlockSpec((qw, qw), c2), pl.BlockSpec((LANES, LANES), c2),
                  pl.BlockSpec((qw, qw), c2), pl.BlockSpec((LANES, LANES), c2),
                  pl.BlockSpec((tm, LANES), lambda b, t: (t, 0)),
                  pl.BlockSpec((tm, LANES), lambda b, t: (t, 0))],
        out_specs=[pl.BlockSpec((1, tm, qw), lambda b, t: (b, t, 0)),
                   pl.BlockSpec((1, tm, LANES), lambda b, t: (b, t, 0)),
                   pl.BlockSpec((1, tm, LANES), lambda b, t: (b, t, 0))],
        out_shape=[jax.ShapeDtypeStruct((B, T, qw), BF16),
                   jax.ShapeDtypeStruct((B, T, LANES), BF16),
                   jax.ShapeDtypeStruct((B, T, LANES), BF16)],
        compiler_params=_cp(("parallel", "parallel")))(x, sc, sh, w, gq, gk, gmq, gmk, pq, pk, cos, sin)


def _gqa_attn_kernel(q_ref, k_ref, v_ref, o_ref, *, nheads, scale):
    k = k_ref[0]
    v = v_ref[0]
    for h in range(nheads):
        sl = slice(h * LANES, (h + 1) * LANES)
        s = _dot_nt(q_ref[0, :, sl], k) * scale
        m = jnp.max(s, axis=-1, keepdims=True)
        p = jnp.exp(s - m)
        den = jnp.sum(p, axis=-1, keepdims=True)
        o = _dot(p.astype(BF16), v)
        o_ref[0, :, sl] = (o / den).astype(o_ref.dtype)


def gqa_attention(q, k, v, tq):
    B, T, QW = q.shape
    Lk = k.shape[1]
    kern = functools.partial(_gqa_attn_kernel, nheads=QW // LANES, scale=HEAD_DIM ** -0.5)
    return pl.pallas_call(
        kern, grid=(B, T // tq),
        in_specs=[pl.BlockSpec((1, tq, QW), lambda b, t: (b, t, 0)),
                  pl.BlockSpec((1, Lk, LANES), lambda b, t: (b, 0, 0)),
                  pl.BlockSpec((1, Lk, LANES), lambda b, t: (b, 0, 0))],
        out_specs=pl.BlockSpec((1, tq, QW), lambda b, t: (b, t, 0)),
        out_shape=jax.ShapeDtypeStruct((B, T, QW), BF16),
        compiler_params=_cp(("parallel", "parallel")))(q, k, v)


def _pair_attention(q2, k_loc, v_loc, bias_pair, k_ctx, v_ctx, scale):
    lane = lax.broadcasted_iota(jnp.int32, q2.shape, 1)
    outs = []
    for half in range(2):
        msk = (lane >= HEAD_DIM) if half else (lane < HEAD_DIM)
        qm = jnp.where(msk, q2, jnp.zeros_like(q2))
        s_ctx = _dot_nt(qm, k_ctx) * scale
        mx = jnp.max(s_ctx, axis=-1, keepdims=True)
        if k_loc is not None:
            s_loc = _dot_nt(qm, k_loc) * scale + bias_pair[half]
            mx = jnp.maximum(mx, jnp.max(s_loc, axis=-1, keepdims=True))
            p_loc = jnp.exp(s_loc - mx)
        p_ctx = jnp.exp(s_ctx - mx)
        den = jnp.sum(p_ctx, axis=-1, keepdims=True)
        o = _dot(p_ctx.astype(BF16), v_ctx)
        if k_loc is not None:
            den = den + jnp.sum(p_loc, axis=-1, keepdims=True)
            o = o + _dot(p_loc.astype(BF16), v_loc)
        outs.append(o / den)
    return jnp.where(lane < HEAD_DIM, outs[0], outs[1])


def _na_kernel(q_ref, k_ref, v_ref, kc_ref, vc_ref, bt_ref, o_ref, *, rows, wr, width, scale):
    r = pl.program_id(1)
    r0 = jnp.clip(r - wr // 2, 0, rows - wr)
    start = pl.multiple_of(r0 * width, width)
    nk = wr * width
    for pr in range(q_ref.shape[2] // LANES):
        sl = slice(pr * LANES, (pr + 1) * LANES)
        o = _pair_attention(q_ref[0, :, sl],
                            k_ref[0, pl.ds(start, nk), sl], v_ref[0, pl.ds(start, nk), sl],
                            (bt_ref[0, 2 * pr], bt_ref[0, 2 * pr + 1]),
                            kc_ref[0, :, sl], vc_ref[0, :, sl], scale)
        o_ref[0, :, sl] = o.astype(o_ref.dtype)


def na_attention(qkv, qkv_c, bias_tab, width, wr):
    B, T, W3 = qkv.shape
    HW = W3 // 3
    C = qkv_c.shape[1]
    rows = T // width
    kern = functools.partial(_na_kernel, rows=rows, wr=wr, width=width, scale=HEAD_DIM ** -0.5)

    def bt_index(b, r):
        r0 = jnp.clip(r - wr // 2, 0, rows - wr)
        return (r0 - r + wr - 1, 0, 0, 0)

    return pl.pallas_call(
        kern, grid=(B, rows),
        in_specs=[pl.BlockSpec((1, width, HW), lambda b, r: (b, r, 0)),
                  pl.BlockSpec((1, T, HW), lambda b, r: (b, 0, 1)),
                  pl.BlockSpec((1, T, HW), lambda b, r: (b, 0, 2)),
                  pl.BlockSpec((1, C, HW), lambda b, r: (b, 0, 1)),
                  pl.BlockSpec((1, C, HW), lambda b, r: (b, 0, 2)),
                  pl.BlockSpec((1,) + bias_tab.shape[1:], bt_index)],
        out_specs=pl.BlockSpec((1, width, HW), lambda b, r: (b, r, 0)),
        out_shape=jax.ShapeDtypeStruct((B, T, HW), BF16),
        compiler_params=_cp(("parallel", "arbitrary")))(qkv, qkv, qkv, qkv_c, qkv_c, bias_tab)


def _ctx_pair_kernel(q_ref, k_ref, v_ref, o_ref, *, scale):
    for pr in range(q_ref.shape[2] // LANES):
        sl = slice(pr * LANES, (pr + 1) * LANES)
        o = _pair_attention(q_ref[0, :, sl], None, None, None, k_ref[0, :, sl], v_ref[0, :, sl], scale)
        o_ref[0, :, sl] = o.astype(o_ref.dtype)


def ctx_pair_attention(qkv_c):
    B, C, W3 = qkv_c.shape
    HW = W3 // 3
    kern = functools.partial(_ctx_pair_kernel, scale=HEAD_DIM ** -0.5)
    return pl.pallas_call(
        kern, grid=(B,),
        in_specs=[pl.BlockSpec((1, C, HW), lambda b: (b, 0, 0)),
                  pl.BlockSpec((1, C, HW), lambda b: (b, 0, 1)),
                  pl.BlockSpec((1, C, HW), lambda b: (b, 0, 2))],
        out_specs=pl.BlockSpec((1, C, HW), lambda b: (b, 0, 0)),
        out_shape=jax.ShapeDtypeStruct((B, C, HW), BF16),
        compiler_params=_cp(("parallel",)))(qkv_c, qkv_c, qkv_c)


def _layer_norm(z, g, b):
    mu = jnp.mean(z, axis=-1, keepdims=True)
    zc = z - mu
    var = jnp.mean(zc * zc, axis=-1, keepdims=True)
    return zc * lax.rsqrt(var + LN_EPS) * g + b


def _merge_kernel(x_ref, sc_ref, sh_ref, g1_ref, ya_ref, yb_ref, yc_ref, wg_ref, wa_ref, wb_ref, wc_ref,
                  wo_ref, lng_ref, lnb_ref, o_ref):
    x = x_ref[0]
    D = x.shape[1]
    h = x * (1.0 + sc_ref[0]) + sh_ref[0]
    gl = jax.nn.sigmoid(_dot(h.astype(BF16), wg_ref[...]))
    m = (gl[:, :D] * _dot(ya_ref[0], wa_ref[...])
         + gl[:, D:2 * D] * _dot(yb_ref[0], wb_ref[...])
         + gl[:, 2 * D:] * _dot(yc_ref[0].astype(BF16), wc_ref[...]))
    y = _dot(m.astype(BF16), wo_ref[...])
    o_ref[0] = _layer_norm(ALPHA * x + g1_ref[0] * y, lng_ref[...], lnb_ref[...])


def merge_block(x, sc, sh, g1, ya, yb, yc, wg, wa, wb, wc, wo, lng, lnb, tm):
    B, T, D = x.shape
    mi = _mod_index(sc.shape[0], B)
    c2 = lambda b, t: (0, 0)
    row = lambda w: pl.BlockSpec((1, tm, w), lambda b, t: (b, t, 0))
    full = lambda a: pl.BlockSpec(a.shape, c2)
    return pl.pallas_call(
        _merge_kernel, grid=(B, T // tm),
        in_specs=[row(D), pl.BlockSpec((1, 1, D), mi), pl.BlockSpec((1, 1, D), mi), pl.BlockSpec((1, 1, D), mi),
                  row(ya.shape[2]), row(yb.shape[2]), row(yc.shape[2]),
                  full(wg), full(wa), full(wb), full(wc), full(wo), full(lng), full(lnb)],
        out_specs=row(D),
        out_shape=jax.ShapeDtypeStruct((B, T, D), F32),
        compiler_params=_cp(("parallel", "parallel")))(x, sc, sh, g1, ya, yb, yc, wg, wa, wb, wc, wo, lng, lnb)


def _topk_rows(s, k):
    R = s.shape[0]
    iota = lax.broadcasted_iota(jnp.int32, s.shape, 0).astype(F32)
    vals, idxs = [], []
    for _ in range(k):
        m = jnp.max(s, axis=0, keepdims=True)
        idx = jnp.min(jnp.where(s == m, iota, float(R)), axis=0, keepdims=True)
        vals.append(m)
        idxs.append(idx)
        s = jnp.where(iota == idx, -jnp.inf, s)
    return jnp.concatenate(vals, axis=0), jnp.concatenate(idxs, axis=0).astype(jnp.int32)


def _pair_candidates(v0, i0, v1, i1, nkeys):
    S = SUBLANES
    vals = [v0[a:a + 1] + v1[:S] for a in range(S)] + [v0[:1] + v1[S:], v0[S:] + v1[:1]]
    eids = [i0[a:a + 1] * nkeys + i1[:S] for a in range(S)] + [i0[:1] * nkeys + i1[S:], i0[S:] * nkeys + i1[:1]]
    cand = jnp.concatenate(vals, axis=0)
    row = lax.broadcasted_iota(jnp.int32, cand.shape, 0)
    slab, b = row >> 3, row & (S - 1)
    flat = jnp.where(slab < S, slab * 16 + b, jnp.where(slab == S, S + b, (S + b) * 16))
    return cand, jnp.concatenate(eids, axis=0), flat.astype(F32)


def _route_kernel(x_ref, sc_ref, sh_ref, wq_ref, sk_ref, h_ref, i_ref, j_ref, g_ref, *, nheads, nkeys, topk):
    h = (x_ref[0] * (1.0 + sc_ref[0]) + sh_ref[0]).astype(BF16)
    h_ref[0] = h
    q = _dot(h, wq_ref[...])
    dk = sk_ref.shape[2]
    i_all, j_all, g_all = [], [], []
    for hd in range(nheads):
        tops = []
        for p in range(2):
            qs = q[:, (2 * hd + p) * dk:(2 * hd + p + 1) * dk].astype(BF16)
            st = _dot_nt(sk_ref[p], qs)
            tops.append(_topk_rows(st, topk))
        (v0, i0), (v1, i1) = tops
        cand, cidx, flat = _pair_candidates(v0, i0, v1, i1, nkeys)
        best, eidx = [], []
        for _ in range(topk):
            m = jnp.max(cand, axis=0, keepdims=True)
            pos = jnp.min(jnp.where(cand == m, flat, float(topk * topk)), axis=0, keepdims=True)
            hit = flat == pos
            best.append(m)
            eidx.append(jnp.sum(jnp.where(hit, cidx, 0), axis=0, keepdims=True))
            cand = jnp.where(hit, -jnp.inf, cand)
        best = jnp.concatenate(best, axis=0)
        e = jnp.concatenate(eidx, axis=0)
        pe = jnp.exp(best - best[0:1])
        g_all.append(pe / jnp.sum(pe, axis=0, keepdims=True))
        i_all.append(e >> (nkeys.bit_length() - 1))
        j_all.append(e & (nkeys - 1))
    i_ref[0] = jnp.concatenate(i_all, axis=0).T
    j_ref[0] = jnp.concatenate(j_all, axis=0).T
    g_ref[0] = jnp.concatenate(g_all, axis=0).T


def peer_route(x, sc, sh, wq, sk, tm):
    B, T, D = x.shape
    mi = _mod_index(sc.shape[0], B)
    npairs = PEER_HEADS * PEER_TOPK
    kern = functools.partial(_route_kernel, nheads=PEER_HEADS, nkeys=sk.shape[1], topk=PEER_TOPK)
    row = lambda w: pl.BlockSpec((1, tm, w), lambda b, t: (b, t, 0))
    return pl.pallas_call(
        kern, grid=(B, T // tm),
        in_specs=[row(D), pl.BlockSpec((1, 1, D), mi), pl.BlockSpec((1, 1, D), mi),
                  pl.BlockSpec(wq.shape, lambda b, t: (0, 0)),
                  pl.BlockSpec(sk.shape, lambda b, t: (0, 0, 0))],
        out_specs=[row(D), row(npairs), row(npairs), row(npairs)],
        out_shape=[jax.ShapeDtypeStruct((B, T, D), BF16),
                   jax.ShapeDtypeStruct((B, T, npairs), jnp.int32),
                   jax.ShapeDtypeStruct((B, T, npairs), jnp.int32),
                   jax.ShapeDtypeStruct((B, T, npairs), F32)],
        compiler_params=_cp(("parallel", "parallel")))(x, sc, sh, wq, sk)


def _peer_act_kernel(h_ref, i_ref, j_ref, g_ref, u_ref, w_ref, a_ref, *, ib, nkeys):
    s_id = pl.program_id(2)

    @pl.when(s_id == 0)
    def _():
        a_ref[...] = jnp.zeros_like(a_ref)

    sc = _dot_nt(h_ref[0], u_ref[...])
    irow = i_ref[0]
    jcol = j_ref[0]
    a = a_ref[...]
    for ii in range(ib):
        got = jnp.take_along_axis(sc[:, ii * nkeys:(ii + 1) * nkeys], jcol, axis=1, mode="promise_in_bounds")
        a = jnp.where(irow == s_id * ib + ii, got, a)
    a_ref[...] = a

    @pl.when(s_id == pl.num_programs(2) - 1)
    def _():
        act = 0.5 * a * (1.0 + lax.erf(a * (2.0 ** -0.5)))
        w_ref[0] = act * g_ref[0]


def peer_act(hb, ii, jj, g, u, tm, ib):
    B, T, D = hb.shape
    npairs = ii.shape[2]
    nkeys = PEER_NKEYS
    steps = u.shape[0] // (ib * nkeys)
    kern = functools.partial(_peer_act_kernel, ib=ib, nkeys=nkeys)
    row = lambda w: pl.BlockSpec((1, tm, w), lambda b, t, s: (b, t, 0))
    return pl.pallas_call(
        kern, grid=(B, T // tm, steps),
        in_specs=[row(D), row(npairs), row(npairs), row(npairs),
                  pl.BlockSpec((ib * nkeys, D), lambda b, t, s: (s, 0))],
        out_specs=row(npairs),
        out_shape=jax.ShapeDtypeStruct((B, T, npairs), F32),
        scratch_shapes=[pltpu.VMEM((tm, npairs), F32)],
        compiler_params=_cp(("parallel", "parallel", "arbitrary")))(hb, ii, jj, g, u)


def _peer_mix_kernel(x_ref, g2_ref, i_ref, j_ref, w_ref, v_ref, lng_ref, lnb_ref, o_ref, p_ref, acc_ref,
                     *, ib, nkeys, pitch):
    s_id = pl.program_id(2)
    tm = x_ref.shape[1]

    @pl.when(s_id == 0)
    def _():
        acc_ref[...] = jnp.zeros_like(acc_ref)
        sub = lax.broadcasted_iota(jnp.int32, (nkeys, i_ref.shape[2]), 0)

        def body(n, carry):
            irow = i_ref[0, pl.ds(n, 1), :]
            jrow = j_ref[0, pl.ds(n, 1), :]
            wrow = w_ref[0, pl.ds(n, 1), :]
            lhs = jnp.where(sub == irow, wrow, 0.0).astype(BF16)
            rhs = jnp.where(sub == jrow, 1.0, 0.0).astype(BF16)
            p_ref[pl.ds(n, nkeys, stride=pitch), :] = _dot_nt(lhs, rhs)
            return carry

        lax.fori_loop(0, tm, body, 0, unroll=MIX_UNROLL)

    parts = [p_ref[pl.ds(pl.multiple_of((s_id * ib + ii) * pitch, 8), tm), :].astype(BF16) for ii in range(ib)]
    acc_ref[...] += _dot(jnp.concatenate(parts, axis=1), v_ref[...])

    @pl.when(s_id == pl.num_programs(2) - 1)
    def _():
        o_ref[0] = _layer_norm(ALPHA * x_ref[0] + g2_ref[0] * acc_ref[...], lng_ref[...], lnb_ref[...])


def peer_mix(x, g2, ii, jj, w, v, lng, lnb, tm, ib):
    B, T, D = x.shape
    npairs = ii.shape[2]
    nkeys = PEER_NKEYS
    steps = v.shape[0] // (ib * nkeys)
    pitch = tm + P_PITCH_PAD
    mi = _mod_index(g2.shape[0], B)
    kern = functools.partial(_peer_mix_kernel, ib=ib, nkeys=nkeys, pitch=pitch)
    row = lambda w_: pl.BlockSpec((1, tm, w_), lambda b, t, s: (b, t, 0))
    c2 = lambda b, t, s: (0, 0)
    return pl.pallas_call(
        kern, grid=(B, T // tm, steps),
        in_specs=[row(D), pl.BlockSpec((1, 1, D), lambda b, t, s: mi(b, t)),
                  row(npairs), row(npairs), row(npairs),
                  pl.BlockSpec((ib * nkeys, D), lambda b, t, s: (s, 0)),
                  pl.BlockSpec(lng.shape, c2), pl.BlockSpec(lnb.shape, c2)],
        out_specs=row(D),
        out_shape=jax.ShapeDtypeStruct((B, T, D), F32),
        scratch_shapes=[pltpu.VMEM((nkeys * pitch, nkeys), F32), pltpu.VMEM((tm, D), F32)],
        compiler_params=_cp(("parallel", "parallel", "arbitrary")))(x, g2, ii, jj, w, v, lng, lnb)


def _hyena_filter(L, w1, b1, fr1, w2, b2, fr2, w3, b3):
    t = jnp.linspace(0.0, 1.0, L, dtype=F32)
    w = 2.0 * math.pi * jnp.arange(L, dtype=F32) / L
    bands = jnp.linspace(1e-4, HY_FILTER_BANDS - 1, HY_FILTER_BANDS, dtype=F32)
    z = jnp.concatenate([t[:, None], jnp.cos(w[:, None] * bands[None]), -jnp.sin(w[:, None] * bands[None])], axis=-1)
    hp = lax.Precision.HIGHEST
    hdn = jnp.sin(fr1 * (jnp.dot(z, w1, precision=hp) + b1))
    hdn = jnp.sin(fr2 * (jnp.dot(hdn, w2, precision=hp) + b2))
    k = (jnp.dot(hdn, w3, precision=hp) + b3).astype(F32).reshape(L, 2, HY_WIDTH)
    max_decay = math.log(HY_DECAY_TARGET) / HY_FAST_DECAY
    min_decay = math.log(HY_DECAY_TARGET) / HY_SLOW_DECAY
    deltas = jnp.linspace(min_decay, max_decay, HY_WIDTH, dtype=F32)
    decay = jnp.exp(-t[:, None] * jnp.abs(deltas)[None, :])
    k = k * decay[:, None, :]
    kfull = jnp.concatenate([k[:, 0], jnp.zeros((1, HY_WIDTH), F32), k[:0:-1, 1]], axis=0)
    return kfull / jnp.sum(jnp.abs(kfull), axis=0, keepdims=True)


def _split_hi_lo(m):
    m32 = jnp.asarray(m, F32)
    hi = m32.astype(BF16)
    return hi, (m32 - hi.astype(F32)).astype(BF16)


def _dot3(m_hi, m_lo, x):
    x_hi = x.astype(BF16)
    x_lo = (x - x_hi.astype(F32)).astype(BF16)
    return _dot(m_hi, x_hi) + _dot(m_hi, x_lo) + _dot(m_lo, x_hi)


def _pad_rows(x, rows):
    if x.shape[0] == rows:
        return x
    return jnp.concatenate([x, jnp.zeros((rows - x.shape[0], x.shape[1]), x.dtype)], axis=0)


def _round_up(n, m):
    return -(-n // m) * m


def _lane_bcast(col):
    return jnp.asarray(np.repeat(np.asarray(col, np.float64).reshape(-1, 1), LANES, axis=1), F32)


def _fft_consts(L):
    N, N2 = 2 * L, HY_N2
    N1 = N // N2
    H1 = N1 // 2
    k1, n1, n2 = np.arange(N1), np.arange(H1), np.arange(N2)
    th = 2 * np.pi * np.outer(k1, n1) / N1
    f1 = np.zeros((2 * N1, _round_up(H1, LANES)))
    f1[:N1, :H1], f1[N1:, :H1] = np.cos(th), -np.sin(th)
    m1 = np.zeros((H1, _round_up(2 * N1, LANES)))
    m1[:, :N1], m1[:, N1:2 * N1] = np.cos(th.T) / N, -np.sin(th.T) / N
    ph = 2 * np.pi * np.outer(n2, n2) / N2
    c, s = np.cos(ph), np.sin(ph)
    m2 = np.block([[c, s], [-s, c]])
    m2i = np.block([[c, -s], [s, c]])
    w = 2 * np.pi / N
    t1 = w * 8 * np.outer(np.arange(N2 // 8), k1)
    t2 = w * np.outer(np.arange(8), k1)
    u1 = w * 8 * np.outer(np.arange(N1 // 8), n2)
    u2 = w * np.outer(np.arange(8), n2)
    tabs = []
    for ang, sign in ((t1, -1.0), (t2, -1.0), (u1, 1.0), (u2, 1.0)):
        tabs += [_lane_bcast(np.cos(ang)), _lane_bcast(sign * np.sin(ang))]
    return dict(N1=N1, mats=_split_hi_lo(f1) + _split_hi_lo(m2) + _split_hi_lo(m2i) + _split_hi_lo(m1), tabs=tuple(tabs))


def _short_conv_block(z, w_ref, b_ref):
    L = z.shape[0]
    row = lax.broadcasted_iota(jnp.int32, z.shape, 0)
    zm = jnp.where(row == 0, 0.0, pltpu.roll(z, 1, axis=0))
    zp = jnp.where(row == L - 1, 0.0, pltpu.roll(z, L - 1, axis=0))
    return zm * w_ref[0:1, :] + z * w_ref[1:2, :] + zp * w_ref[2:3, :] + b_ref[...]


def _cmul(ar, ai, br, bi):
    return ar * br - ai * bi, ar * bi + ai * br


def _fft_stage1(src_ref, as_ref, f1h, f1l, t1r, t1i, t2r, t2i, *, N1, N2):
    H1 = N1 // 2
    pa = 2 * N1 + FFT_PITCH_PAD
    kp = f1h.shape[1]

    def body(n2, carry):
        xs = _pad_rows(src_ref[pl.ds(n2, H1, stride=N2), :], kp)
        a = _dot3(f1h[...], f1l[...], xs)
        oa = pl.multiple_of((n2 >> 3) * N1, SUBLANES)
        ob = pl.multiple_of((n2 & 7) * N1, SUBLANES)
        tr, ti = _cmul(t1r[pl.ds(oa, N1), :], t1i[pl.ds(oa, N1), :], t2r[pl.ds(ob, N1), :], t2i[pl.ds(ob, N1), :])
        yr, yi = _cmul(a[:N1], a[N1:], tr, ti)
        base = pl.multiple_of(n2 * pa, SUBLANES)
        as_ref[pl.ds(base, N1), :] = yr
        as_ref[pl.ds(base + N1, N1), :] = yi
        return carry

    lax.fori_loop(0, N2, body, 0)


def _fft_stage2(as_ref, k1, m2h, m2l, *, N1, N2):
    pa = 2 * N1 + FFT_PITCH_PAD
    v = jnp.concatenate([as_ref[pl.ds(k1, N2, stride=pa), :], as_ref[pl.ds(N1 + k1, N2, stride=pa), :]], axis=0)
    return _dot3(m2h[...], m2l[...], v)


def _hy_spec_kernel(kf_ref, kb_ref, f1h, f1l, m2h, m2l, t1r, t1i, t2r, t2i, o_ref, as_ref, *, N1, N2):
    for part, src in enumerate((kf_ref, kb_ref)):
        _fft_stage1(src, as_ref, f1h, f1l, t1r, t1i, t2r, t2i, N1=N1, N2=N2)

        def body(k1, carry):
            x = _fft_stage2(as_ref, k1, m2h, m2l, N1=N1, N2=N2)
            if part == 0:
                o_ref[k1] = x
            else:
                sign = (1 - 2 * (k1 & 1)).astype(F32)
                o_ref[k1] = o_ref[k1] + sign * x
            return carry

        lax.fori_loop(0, N1, body, 0)


def hy_spectrum(kf, kb, fc):
    L, CH = kf.shape
    N1, N2, C = fc["N1"], HY_N2, HY_CB
    f1h, f1l, m2h, m2l = fc["mats"][:4]
    tabs = fc["tabs"][:4]
    kern = functools.partial(_hy_spec_kernel, N1=N1, N2=N2)
    const = lambda a: pl.BlockSpec(a.shape, lambda j: (0, 0))
    return pl.pallas_call(
        kern, grid=(CH // C,),
        in_specs=[pl.BlockSpec((L, C), lambda j: (0, j)), pl.BlockSpec((L, C), lambda j: (0, j))]
        + [const(a) for a in (f1h, f1l, m2h, m2l) + tuple(tabs)],
        out_specs=pl.BlockSpec((N1, 2 * N2, C), lambda j: (0, 0, j)),
        out_shape=jax.ShapeDtypeStruct((N1, 2 * N2, CH), F32),
        scratch_shapes=[pltpu.VMEM((N2 * (2 * N1 + FFT_PITCH_PAD), C), F32)],
        compiler_params=_cp(("parallel",)))(kf, kb, f1h, f1l, m2h, m2l, *tabs)


def _hy_conv_kernel(x0_ref, x1_ref, v_ref, w0_ref, w1_ref, wv_ref, b0_ref, b1_ref, bv_ref, hb_ref, ksp_ref,
                    f1h, f1l, m2h, m2l, m2ih, m2il, m1h, m1l, t1r, t1i, t2r, t2i, u1r, u1i, u2r, u2i,
                    o_ref, uv_ref, g_ref, as_ref, bs_ref, *, N1, N2):
    H1 = N1 // 2
    pb = 2 * N2 + FFT_PITCH_PAD
    uv_ref[...] = _short_conv_block(v_ref[0], wv_ref, bv_ref) * _short_conv_block(x1_ref[0], w1_ref, b1_ref)
    g_ref[...] = _short_conv_block(x0_ref[0], w0_ref, b0_ref)
    _fft_stage1(uv_ref, as_ref, f1h, f1l, t1r, t1i, t2r, t2i, N1=N1, N2=N2)

    def freq_body(k1, carry):
        x = _fft_stage2(as_ref, k1, m2h, m2l, N1=N1, N2=N2)
        ks = ksp_ref[k1]
        yr, yi = _cmul(x[:N2], x[N2:], ks[:N2], ks[N2:])
        b = _dot3(m2ih[...], m2il[...], jnp.concatenate([yr, yi], axis=0))
        oa = pl.multiple_of((k1 >> 3) * N2, SUBLANES)
        ob = pl.multiple_of((k1 & 7) * N2, SUBLANES)
        ur, ui = _cmul(u1r[pl.ds(oa, N2), :], u1i[pl.ds(oa, N2), :], u2r[pl.ds(ob, N2), :], u2i[pl.ds(ob, N2), :])
        br, bi = _cmul(b[:N2], b[N2:], ur, ui)
        base = pl.multiple_of(k1 * pb, SUBLANES)
        bs_ref[pl.ds(base, N2), :] = br
        bs_ref[pl.ds(base + N2, N2), :] = bi
        return carry

    lax.fori_loop(0, N1, freq_body, 0)
    kp = m1h.shape[1]

    def time_body(n2, carry):
        w = jnp.concatenate([bs_ref[pl.ds(n2, N1, stride=pb), :], bs_ref[pl.ds(N2 + n2, N1, stride=pb), :]], axis=0)
        y = _dot3(m1h[...], m1l[...], _pad_rows(w, kp))
        rows = pl.ds(n2, H1, stride=N2)
        o_ref[0, rows, :] = (y + uv_ref[rows, :] * hb_ref[...]) * g_ref[rows, :]
        return carry

    lax.fori_loop(0, N2, time_body, 0)


def hy_conv(zh, short_w, short_b, hy_bias, ksp, fc):
    B, L, W3 = zh.shape
    CH = W3 // 3
    N1, N2, C = fc["N1"], HY_N2, HY_CB
    nb = CH // C
    kern = functools.partial(_hy_conv_kernel, N1=N1, N2=N2)
    sync = pl.Buffered(1)
    zspec = lambda g: pl.BlockSpec((1, L, C), lambda j, b: (b, 0, g * nb + j), pipeline_mode=sync)
    wspec = lambda g: pl.BlockSpec((HY_SHORT, C), lambda j, b: (0, g * nb + j))
    bspec = lambda g: pl.BlockSpec((1, C), lambda j, b: (0, g * nb + j))
    const = lambda a: pl.BlockSpec(a.shape, lambda j, b: (0, 0), pipeline_mode=sync)
    consts = fc["mats"] + fc["tabs"]
    return pl.pallas_call(
        kern, grid=(nb, B),
        in_specs=[zspec(0), zspec(1), zspec(2), wspec(0), wspec(1), wspec(2), bspec(0), bspec(1), bspec(2),
                  pl.BlockSpec((1, C), lambda j, b: (0, j)),
                  pl.BlockSpec((N1, 2 * N2, C), lambda j, b: (0, 0, j), pipeline_mode=sync)]
        + [const(a) for a in consts],
        out_specs=pl.BlockSpec((1, L, C), lambda j, b: (b, 0, j)),
        out_shape=jax.ShapeDtypeStruct((B, L, CH), F32),
        scratch_shapes=[pltpu.VMEM((L, C), F32), pltpu.VMEM((L, C), F32),
                        pltpu.VMEM((N2 * (2 * N1 + FFT_PITCH_PAD), C), F32),
                        pltpu.VMEM((N1 * (2 * N2 + FFT_PITCH_PAD), C), F32)],
        compiler_params=_cp(("parallel", "parallel")))(
            zh, zh, zh, short_w, short_w, short_w, short_b, short_b, short_b, hy_bias, ksp, *consts)


def _dft_consts(L):
    N = 2 * L
    k = np.arange(N)
    ang = 2 * np.pi * np.outer(k, k) / N
    fwd = np.concatenate([np.cos(ang), -np.sin(ang)], axis=0)
    inv = np.concatenate([np.cos(ang[:L]), -np.sin(ang[:L])], axis=1) / N
    return _split_hi_lo(fwd[:, :L]) + _split_hi_lo(fwd) + _split_hi_lo(inv)


def _hy_small_kernel(x0_ref, x1_ref, v_ref, w0_ref, w1_ref, wv_ref, b0_ref, b1_ref, bv_ref, hb_ref, kfull_ref,
                     fh, fl, ffh, ffl, ih, il, o_ref):
    N = kfull_ref.shape[0]
    uv = _short_conv_block(v_ref[0], wv_ref, bv_ref) * _short_conv_block(x1_ref[0], w1_ref, b1_ref)
    g = _short_conv_block(x0_ref[0], w0_ref, b0_ref)
    x = _dot3(fh[...], fl[...], uv)
    ks = _dot3(ffh[...], ffl[...], kfull_ref[...])
    yr, yi = _cmul(x[:N], x[N:], ks[:N], ks[N:])
    y = _dot3(ih[...], il[...], jnp.concatenate([yr, yi], axis=0))
    o_ref[0] = (y + uv * hb_ref[...]) * g


def hy_conv_small(zh, short_w, short_b, hy_bias, kfull, dc):
    B, L, W3 = zh.shape
    CH = W3 // 3
    C = HY_CB
    nb = CH // C
    zspec = lambda g: pl.BlockSpec((1, L, C), lambda j, b: (b, 0, g * nb + j))
    wspec = lambda g: pl.BlockSpec((HY_SHORT, C), lambda j, b: (0, g * nb + j))
    bspec = lambda g: pl.BlockSpec((1, C), lambda j, b: (0, g * nb + j))
    const = lambda a: pl.BlockSpec(a.shape, lambda j, b: (0, 0))
    return pl.pallas_call(
        _hy_small_kernel, grid=(nb, B),
        in_specs=[zspec(0), zspec(1), zspec(2), wspec(0), wspec(1), wspec(2), bspec(0), bspec(1), bspec(2),
                  pl.BlockSpec((1, C), lambda j, b: (0, j)),
                  pl.BlockSpec((2 * L, C), lambda j, b: (0, j))]
        + [const(a) for a in dc],
        out_specs=pl.BlockSpec((1, L, C), lambda j, b: (b, 0, j)),
        out_shape=jax.ShapeDtypeStruct((B, L, CH), F32),
        compiler_params=_cp(("parallel", "parallel")))(
            zh, zh, zh, short_w, short_w, short_w, short_b, short_b, short_b, hy_bias, kfull, *dc)


def _gqa_slot_offsets():
    group = GQA_HEADS // GQA_KV_HEADS
    return [(h * LANES + (h // group) * HEAD_DIM) for h in range(GQA_HEADS)]


def _pad_gqa_cols(w):
    out = jnp.zeros((w.shape[0], GQA_HEADS * LANES), w.dtype)
    for h, off in enumerate(_gqa_slot_offsets()):
        out = out.at[:, off:off + HEAD_DIM].set(w[:, h * HEAD_DIM:(h + 1) * HEAD_DIM])
    return out


def _pad_gqa_vec(g):
    out = jnp.zeros((GQA_HEADS * LANES,), g.dtype)
    for off in _gqa_slot_offsets():
        out = out.at[off:off + HEAD_DIM].set(g)
    return out


def _rope_consts(T):
    half = HEAD_DIM // 2
    quarter = half // 2
    t = jnp.arange(T)
    inv = ROPE_THETA ** (-jnp.arange(0, half, 2, dtype=F32) / half)
    ar = (t // GRID_W).astype(F32)[:, None] * inv[None, :]
    ac = (t % GRID_W).astype(F32)[:, None] * inv[None, :]
    ang = jnp.concatenate([ar, ar, ac, ac] * (LANES // HEAD_DIM), axis=1)
    def perm(width, valid):
        P = np.zeros((width, width), np.float32)
        for o in range(0, width, half):
            if not valid(o):
                continue
            for j in range(quarter):
                P[o + j + quarter, o + j] = -1.0
                P[o + j, o + j + quarter] = 1.0
        return jnp.asarray(P, BF16)
    offs = _gqa_slot_offsets()
    qvalid = lambda o: any(off <= o < off + HEAD_DIM for off in offs)
    return jnp.cos(ang), jnp.sin(ang), perm(GQA_HEADS * LANES, qvalid), perm(LANES, lambda o: True)


def _group_mean_mats():
    qw = GQA_HEADS * LANES
    gq = np.zeros((qw, qw), np.float32)
    for s in range(0, qw, LANES):
        gq[s:s + LANES, s:s + LANES] = 1.0 / HEAD_DIM
    gk = np.zeros((LANES, LANES), np.float32)
    for s in range(0, LANES, HEAD_DIM):
        gk[s:s + HEAD_DIM, s:s + HEAD_DIM] = 1.0 / HEAD_DIM
    return jnp.asarray(gq, BF16), jnp.asarray(gk, BF16)


def _na_bias_table(rpb, rows):
    wr = min(NA_WIN_ROWS, rows)
    wc = NA_WIN_COLS
    q = np.arange(GRID_W)
    c0 = np.clip(q - wc // 2, 0, GRID_W - wc)
    kc = np.arange(GRID_W)
    valid = (kc[None, :] >= c0[:, None]) & (kc[None, :] < c0[:, None] + wc)
    dc = np.clip(kc[None, :] - q[:, None] + (NA_WIN_COLS - 1), 0, 2 * NA_WIN_COLS - 2)
    assert wr == NA_WIN_ROWS
    H, ncols = rpb.shape[0], rpb.shape[2]
    onehot = (dc[None] == np.arange(ncols)[:, None, None]) & valid[None]
    tq = jnp.einsum('hdc,cqk->hdqk', rpb, jnp.asarray(onehot, F32), precision=lax.Precision.HIGHEST)
    tq = jnp.where(jnp.asarray(valid)[None, None], tq, NEG)
    slabs = [jnp.transpose(tq[:, o:o + wr], (0, 2, 1, 3)).reshape(H, GRID_W, wr * GRID_W) for o in range(wr)]
    return jnp.stack(slabs, axis=0)


def kernel(x, c, ctx, c_ctx, w_ada, b_ada, w_in, na_rpb, gqa_q_gain, gqa_k_gain, hy_short_w, hy_short_b,
           hy_f_w1, hy_f_b1, hy_f_freq1, hy_f_w2, hy_f_b2, hy_f_freq2, hy_f_w3, hy_f_b3, hy_bias,
           w_br_a, w_br_b, w_br_c, w_out, ln1_g, ln1_b, peer_w_q, peer_sub_keys, peer_u, peer_v, ln2_g, ln2_b):
    B, T, D = x.shape
    C = ctx.shape[1]
    depth = w_ada.shape[0]
    rows = T // GRID_W
    wr = min(NA_WIN_ROWS, rows)
    NAW = NA_HEADS * HEAD_DIM
    GQW = GQA_HEADS * HEAD_DIM
    GKW = GQA_KV_HEADS * HEAD_DIM
    o_na, o_gq, o_gk, o_gv, o_hy, o_gl = 0, 3 * NAW, 3 * NAW + GQW, 3 * NAW + GQW + GKW, 3 * NAW + GQW + 2 * GKW, \
        3 * NAW + GQW + 2 * GKW + 3 * HY_WIDTH

    cos_t, sin_t, perm_q, perm_k = _rope_consts(T)
    gm_q, gm_k = _group_mean_mats()
    fft_c = _fft_consts(T)
    dft_c = _dft_consts(C)
    c_all = jnp.concatenate([c, c_ctx[None, :], jnp.zeros((16 - B - 1, D), F32)], axis=0)
    tm = min(512, T)
    tmc = C

    for l in range(depth):
        need_ctx = l < depth - 1
        mod = ada_mod(c_all, w_ada[l], b_ada[l][None, :])
        ml = mod[:B].reshape(B, 1, 6, D)
        mc = mod[B:B + 1].reshape(1, 1, 6, D)
        sh1, sc1, g1, sh2, sc2, g2 = [ml[:, :, i] for i in range(6)]
        csh1, csc1, cg1, csh2, csc2, cg2 = [mc[:, :, i] for i in range(6)]

        wl = w_in[l]
        w_na = wl[:, o_na:o_gq].astype(BF16)
        w_gq = jnp.concatenate([_pad_gqa_cols(wl[:, o_gq:o_gk]), wl[:, o_gk:o_hy]], axis=1).astype(BF16)
        w_hy = wl[:, o_hy:o_gl].astype(BF16)
        w_gl = wl[:, o_gl:].astype(BF16)
        gq = _pad_gqa_vec(gqa_q_gain[l])[None, :]
        gk = jnp.tile(gqa_k_gain[l], GQA_KV_HEADS)[None, :]
        wa = w_br_a[l].astype(BF16)
        wb = _pad_gqa_cols(w_br_b[l].T).T.astype(BF16)
        wc = w_br_c[l].astype(BF16)
        wo = w_out[l].astype(BF16)
        lng1, lnb1 = ln1_g[l][None, :], ln1_b[l][None, :]
        lng2, lnb2 = ln2_g[l][None, :], ln2_b[l][None, :]
        wq = peer_w_q[l].astype(BF16)
        sk = peer_sub_keys[l].astype(BF16)
        ub = peer_u[l].astype(BF16)
        vb = peer_v[l].astype(BF16)
        bias_tab = _na_bias_table(na_rpb[l], rows)

        qkv_a = mod_matmul(x, sc1, sh1, w_na, BF16, tm)
        qkv_ac = mod_matmul(ctx, csc1, csh1, w_na, BF16, tmc)
        ya = na_attention(qkv_a, qkv_ac, bias_tab, GRID_W, wr)

        qb, kb, vb_ = gqa_proj(x, sc1, sh1, w_gq, gq, gk, gm_q, gm_k, perm_q, perm_k, cos_t, sin_t, True, tm)
        qbc, kbc, vbc = gqa_proj(ctx, csc1, csh1, w_gq, gq, gk, gm_q, gm_k, perm_q, perm_k,
                                 cos_t[:C], sin_t[:C], False, tmc)
        yb = gqa_attention(qb, jnp.concatenate([kb, kbc], axis=1), jnp.concatenate([vb_, vbc], axis=1), min(256, T))

        zh = mod_matmul(x, sc1, sh1, w_hy, F32, tm)
        filt = functools.partial(_hyena_filter, w1=hy_f_w1[l], b1=hy_f_b1[l], fr1=hy_f_freq1[l], w2=hy_f_w2[l],
                                 b2=hy_f_b2[l], fr2=hy_f_freq2[l], w3=hy_f_w3[l], b3=hy_f_b3[l])
        kfull = filt(T)
        ksp = hy_spectrum(kfull[:T], kfull[T:], fft_c)
        yc = hy_conv(zh, hy_short_w[l], hy_short_b[l][None, :], hy_bias[l][None, :], ksp, fft_c)

        x_mid = merge_block(x, sc1, sh1, g1, ya, yb, yc, w_gl, wa, wb, wc, wo, lng1, lnb1, min(256, T))

        hb, ii, jj, gg = peer_route(x_mid, sc2, sh2, wq, sk, min(256, T))
        wgt = peer_act(hb, ii, jj, gg, ub, min(1024, T), 8)
        x_new = peer_mix(x_mid, g2, ii, jj, wgt, vb, lng2, lnb2, min(256, T), 16)

        if need_ctx:
            yac = ctx_pair_attention(qkv_ac)
            ybc = gqa_attention(qbc, kbc, vbc, C)
            zhc = mod_matmul(ctx, csc1, csh1, w_hy, F32, tmc)
            ycc = hy_conv_small(zhc, hy_short_w[l], hy_short_b[l][None, :], hy_bias[l][None, :], filt(C), dft_c)
            c_mid = merge_block(ctx, csc1, csh1, cg1, yac, ybc, ycc, w_gl, wa, wb, wc, wo, lng1, lnb1, tmc)
            hbc, iic, jjc, ggc = peer_route(c_mid, csc2, csh2, wq, sk, tmc)
            wgtc = peer_act(hbc, iic, jjc, ggc, ub, tmc, 8)
            ctx = peer_mix(c_mid, cg2, iic, jjc, wgtc, vb, lng2, lnb2, tmc, 16)
        x = x_new
    return x
```

```python
import functools
import math

import numpy as np
import jax
import jax.numpy as jnp
from jax import lax
from jax.experimental import pallas as pl
from jax.experimental.pallas import tpu as pltpu

F32 = jnp.float32
BF16 = jnp.bfloat16

GRID_W = 64
NA_HEADS = 8
NA_WIN_ROWS = 8
NA_WIN_COLS = 16
GQA_HEADS = 8
GQA_KV_HEADS = 2
HEAD_DIM = 64
ROPE_THETA = 10000.0
HY_WIDTH = 512
HY_SHORT = 3
HY_FILTER_BANDS = 8
HY_FAST_DECAY = 0.3
HY_SLOW_DECAY = 1.5
HY_DECAY_TARGET = 1e-2
PEER_HEADS = 8
PEER_NKEYS = 128
PEER_TOPK = 16
DEPTH = 4
ALPHA = (2 * DEPTH) ** 0.25
LN_EPS = 1e-5
RMS_EPS = 1e-6

LANES = 128
SUBLANES = 8
HY_N2 = 128
HY_CB = LANES
FFT_PITCH_PAD = 8
FFT_TIME_GROUP = 4
FFT_FREQ_GROUP = 2
FFT_UNROLL = 4
MIX_UNROLL = 8
VMEM_LIMIT = 56 * 1024 * 1024
P_PITCH_PAD = 8
NEG = -1e30


def _cp(sem, vmem=VMEM_LIMIT):
    return pltpu.CompilerParams(dimension_semantics=sem, vmem_limit_bytes=vmem)


def _dot(a, b):
    return jnp.dot(a, b, preferred_element_type=F32)


def _dot_nt(a, b):
    return lax.dot_general(a, b, (((1,), (1,)), ((), ())), preferred_element_type=F32)


def _dot_split(a, b):
    a0 = a.astype(BF16)
    r1 = a - a0.astype(F32)
    a1 = r1.astype(BF16)
    a2 = (r1 - a1.astype(F32)).astype(BF16)
    return _dot(a0, b) + _dot(a1, b) + _dot(a2, b)


def _ada_kernel(c_ref, w_ref, b_ref, o_ref):
    c = c_ref[...]
    s = c * jax.nn.sigmoid(c)
    o_ref[...] = jnp.dot(s, w_ref[...], preferred_element_type=F32,
                         precision=lax.Precision.HIGHEST) + b_ref[...]


def ada_mod(c_all, w, b):
    R, D = c_all.shape
    N = w.shape[1]
    tn = 1024 if N % 1024 == 0 else N
    return pl.pallas_call(
        _ada_kernel, grid=(N // tn,),
        in_specs=[pl.BlockSpec((R, D), lambda j: (0, 0)),
                  pl.BlockSpec((D, tn), lambda j: (0, j)),
                  pl.BlockSpec((1, tn), lambda j: (0, j))],
        out_specs=pl.BlockSpec((R, tn), lambda j: (0, j)),
        out_shape=jax.ShapeDtypeStruct((R, N), F32),
        compiler_params=_cp(("arbitrary",)))(c_all, w, b)


def _modmm_kernel(x_ref, sc_ref, sh_ref, w_ref, o_ref):
    h = x_ref[0] * (1.0 + sc_ref[0]) + sh_ref[0]
    o_ref[0] = _dot(h.astype(BF16), w_ref[...]).astype(o_ref.dtype)


def _mod_index(Bm, B):
    if Bm == B:
        return lambda b, t: (b, 0, 0)
    return lambda b, t: (0, 0, 0)


def mod_matmul(x, sc, sh, w, out_dtype, tm):
    B, T, D = x.shape
    N = w.shape[1]
    mi = _mod_index(sc.shape[0], B)
    return pl.pallas_call(
        _modmm_kernel, grid=(B, T // tm),
        in_specs=[pl.BlockSpec((1, tm, D), lambda b, t: (b, t, 0)),
                  pl.BlockSpec((1, 1, D), mi), pl.BlockSpec((1, 1, D), mi),
                  pl.BlockSpec((D, N), lambda b, t: (0, 0))],
        out_specs=pl.BlockSpec((1, tm, N), lambda b, t: (b, t, 0)),
        out_shape=jax.ShapeDtypeStruct((B, T, N), out_dtype),
        compiler_params=_cp(("parallel", "parallel")))(x, sc, sh, w)


def _gqa_proj_kernel(x_ref, sc_ref, sh_ref, w_ref, gain_ref, gm_ref, perm_ref, cos_ref, sin_ref,
                     q_ref, k_ref, v_ref, *, rope, qw):
    h = x_ref[0] * (1.0 + sc_ref[0]) + sh_ref[0]
    z = _dot(h.astype(BF16), w_ref[...])
    outs = []
    for s in range(qw // LANES + 1):
        sl = slice(s * LANES, (s + 1) * LANES)
        y = z[:, sl]
        y = y * lax.rsqrt(_dot_split(y * y, gm_ref[...]) + RMS_EPS) * gain_ref[:, sl]
        if rope:
            y = y * cos_ref[...] + _dot_split(y, perm_ref[...]) * sin_ref[...]
        outs.append(y)
    q_ref[0] = jnp.concatenate(outs[:-1], axis=1).astype(q_ref.dtype)
    k_ref[0] = outs[-1].astype(k_ref.dtype)
    v_ref[0] = z[:, qw + LANES:].astype(v_ref.dtype)


def gqa_proj(x, sc, sh, w, gain, gm, perm, cos, sin, rope, tm):
    B, T, D = x.shape
    N = w.shape[1]
    qw = N - 2 * LANES
    mi = _mod_index(sc.shape[0], B)
    c2 = lambda b, t: (0, 0)
    kern = functools.partial(_gqa_proj_kernel, rope=rope, qw=qw)
    return pl.pallas_call(
        kern, grid=(B, T // tm),
        in_specs=[pl.BlockSpec((1, tm, D), lambda b, t: (b, t, 0)),
                  pl.BlockSpec((1, 1, D), mi), pl.BlockSpec((1, 1, D), mi),
                  pl.BlockSpec((D, N), c2),
                  pl.BlockSpec((1, qw + LANES), c2),
                  pl.BlockSpec((LANES, LANES), c2), pl.BlockSpec((LANES, LANES), c2),
                  pl.BlockSpec((tm, LANES), lambda b, t: (t, 0)),
                  pl.BlockSpec((tm, LANES), lambda b, t: (t, 0))],
        out_specs=[pl.BlockSpec((1, tm, qw), lambda b, t: (b, t, 0)),
                   pl.BlockSpec((1, tm, LANES), lambda b, t: (b, t, 0)),
                   pl.BlockSpec((1, tm, LANES), lambda b, t: (b, t, 0))],
        out_shape=[jax.ShapeDtypeStruct((B, T, qw), BF16),
                   jax.ShapeDtypeStruct((B, T, LANES), BF16),
                   jax.ShapeDtypeStruct((B, T, LANES), BF16)],
        compiler_params=_cp(("parallel", "parallel")))(x, sc, sh, w, gain, gm, perm, cos, sin)


def _gqa_attn_kernel(q_ref, k_ref, v_ref, o_ref, *, nheads, scale):
    k = k_ref[0]
    v = v_ref[0]
    for h in range(nheads):
        sl = slice(h * LANES, (h + 1) * LANES)
        s = _dot_nt(q_ref[0, :, sl], k) * scale
        m = jnp.max(s, axis=-1, keepdims=True)
        p = jnp.exp(s - m)
        den = jnp.sum(p, axis=-1, keepdims=True)
        o = _dot(p.astype(BF16), v)
        o_ref[0, :, sl] = (o / den).astype(o_ref.dtype)


def gqa_attention(q, k, v, tq):
    B, T, QW = q.shape
    Lk = k.shape[1]
    kern = functools.partial(_gqa_attn_kernel, nheads=QW // LANES, scale=HEAD_DIM ** -0.5)
    return pl.pallas_call(
        kern, grid=(B, T // tq),
        in_specs=[pl.BlockSpec((1, tq, QW), lambda b, t: (b, t, 0)),
                  pl.BlockSpec((1, Lk, LANES), lambda b, t: (b, 0, 0)),
                  pl.BlockSpec((1, Lk, LANES), lambda b, t: (b, 0, 0))],
        out_specs=pl.BlockSpec((1, tq, QW), lambda b, t: (b, t, 0)),
        out_shape=jax.ShapeDtypeStruct((B, T, QW), BF16),
        compiler_params=_cp(("parallel", "parallel")))(q, k, v)


def _pair_attention(q2, k_loc, v_loc, bias_pair, k_ctx, v_ctx, scale):
    Lq = q2.shape[0]
    low = lax.broadcasted_iota(jnp.int32, q2.shape, 1) < HEAD_DIM
    zero = jnp.zeros_like(q2)
    qs = jnp.concatenate([jnp.where(low, q2, zero), jnp.where(low, zero, q2)], axis=0)
    s_ctx = _dot_nt(qs, k_ctx) * scale
    mx = jnp.max(s_ctx, axis=-1, keepdims=True)
    if k_loc is not None:
        s_loc = _dot_nt(qs, k_loc) * scale + jnp.concatenate(bias_pair, axis=0)
        mx = jnp.maximum(mx, jnp.max(s_loc, axis=-1, keepdims=True))
        p_loc = jnp.exp(s_loc - mx)
    p_ctx = jnp.exp(s_ctx - mx)
    den = jnp.sum(p_ctx, axis=-1, keepdims=True)
    o = _dot(p_ctx.astype(BF16), v_ctx)
    if k_loc is not None:
        den = den + jnp.sum(p_loc, axis=-1, keepdims=True)
        o = o + _dot(p_loc.astype(BF16), v_loc)
    o = o / den
    return jnp.where(low, o[:Lq], o[Lq:])


def _na_kernel(q_ref, k_ref, v_ref, kc_ref, vc_ref, bt_ref, o_ref, *, rows, wr, width, scale):
    r = pl.program_id(1)
    r0 = jnp.clip(r - wr // 2, 0, rows - wr)
    start = pl.multiple_of(r0 * width, width)
    nk = wr * width
    for pr in range(q_ref.shape[2] // LANES):
        sl = slice(pr * LANES, (pr + 1) * LANES)
        o = _pair_attention(q_ref[0, :, sl],
                            k_ref[0, pl.ds(start, nk), sl], v_ref[0, pl.ds(start, nk), sl],
                            (bt_ref[0, 2 * pr], bt_ref[0, 2 * pr + 1]),
                            kc_ref[0, :, sl], vc_ref[0, :, sl], scale)
        o_ref[0, :, sl] = o.astype(o_ref.dtype)


def na_attention(qkv, qkv_c, bias_tab, width, wr):
    B, T, W3 = qkv.shape
    HW = W3 // 3
    C = qkv_c.shape[1]
    rows = T // width
    kern = functools.partial(_na_kernel, rows=rows, wr=wr, width=width, scale=HEAD_DIM ** -0.5)

    def bt_index(b, r):
        r0 = jnp.clip(r - wr // 2, 0, rows - wr)
        return (r0 - r + wr - 1, 0, 0, 0)

    return pl.pallas_call(
        kern, grid=(B, rows),
        in_specs=[pl.BlockSpec((1, width, HW), lambda b, r: (b, r, 0)),
                  pl.BlockSpec((1, T, HW), lambda b, r: (b, 0, 1)),
                  pl.BlockSpec((1, T, HW), lambda b, r: (b, 0, 2)),
                  pl.BlockSpec((1, C, HW), lambda b, r: (b, 0, 1)),
                  pl.BlockSpec((1, C, HW), lambda b, r: (b, 0, 2)),
                  pl.BlockSpec((1,) + bias_tab.shape[1:], bt_index)],
        out_specs=pl.BlockSpec((1, width, HW), lambda b, r: (b, r, 0)),
        out_shape=jax.ShapeDtypeStruct((B, T, HW), BF16),
        compiler_params=_cp(("parallel", "arbitrary")))(qkv, qkv, qkv, qkv_c, qkv_c, bias_tab)


def _ctx_pair_kernel(q_ref, k_ref, v_ref, o_ref, *, scale):
    for pr in range(q_ref.shape[2] // LANES):
        sl = slice(pr * LANES, (pr + 1) * LANES)
        o = _pair_attention(q_ref[0, :, sl], None, None, None, k_ref[0, :, sl], v_ref[0, :, sl], scale)
        o_ref[0, :, sl] = o.astype(o_ref.dtype)


def ctx_pair_attention(qkv_c):
    B, C, W3 = qkv_c.shape
    HW = W3 // 3
    kern = functools.partial(_ctx_pair_kernel, scale=HEAD_DIM ** -0.5)
    return pl.pallas_call(
        kern, grid=(B,),
        in_specs=[pl.BlockSpec((1, C, HW), lambda b: (b, 0, 0)),
                  pl.BlockSpec((1, C, HW), lambda b: (b, 0, 1)),
                  pl.BlockSpec((1, C, HW), lambda b: (b, 0, 2))],
        out_specs=pl.BlockSpec((1, C, HW), lambda b: (b, 0, 0)),
        out_shape=jax.ShapeDtypeStruct((B, C, HW), BF16),
        compiler_params=_cp(("parallel",)))(qkv_c, qkv_c, qkv_c)


def _layer_norm(z, g, b):
    mu = jnp.mean(z, axis=-1, keepdims=True)
    zc = z - mu
    var = jnp.mean(zc * zc, axis=-1, keepdims=True)
    return zc * lax.rsqrt(var + LN_EPS) * g + b


def _merge_kernel(x_ref, sc_ref, sh_ref, g1_ref, ya_ref, yb_ref, yc_ref, wg_ref, wa_ref, wb_ref, wc_ref,
                  wo_ref, lng_ref, lnb_ref, o_ref):
    x = x_ref[0]
    D = x.shape[1]
    h = x * (1.0 + sc_ref[0]) + sh_ref[0]
    gl = jax.nn.sigmoid(_dot(h.astype(BF16), wg_ref[...]))
    m = (gl[:, :D] * _dot(ya_ref[0], wa_ref[...])
         + gl[:, D:2 * D] * _dot(yb_ref[0], wb_ref[...])
         + gl[:, 2 * D:] * _dot(yc_ref[0].astype(BF16), wc_ref[...]))
    y = _dot(m.astype(BF16), wo_ref[...])
    o_ref[0] = _layer_norm(ALPHA * x + g1_ref[0] * y, lng_ref[...], lnb_ref[...])


def merge_block(x, sc, sh, g1, ya, yb, yc, wg, wa, wb, wc, wo, lng, lnb, tm):
    B, T, D = x.shape
    mi = _mod_index(sc.shape[0], B)
    c2 = lambda b, t: (0, 0)
    row = lambda w: pl.BlockSpec((1, tm, w), lambda b, t: (b, t, 0))
    full = lambda a: pl.BlockSpec(a.shape, c2)
    return pl.pallas_call(
        _merge_kernel, grid=(B, T // tm),
        in_specs=[row(D), pl.BlockSpec((1, 1, D), mi), pl.BlockSpec((1, 1, D), mi), pl.BlockSpec((1, 1, D), mi),
                  row(ya.shape[2]), row(yb.shape[2]), row(yc.shape[2]),
                  full(wg), full(wa), full(wb), full(wc), full(wo), full(lng), full(lnb)],
        out_specs=row(D),
        out_shape=jax.ShapeDtypeStruct((B, T, D), F32),
        compiler_params=_cp(("parallel", "parallel")))(x, sc, sh, g1, ya, yb, yc, wg, wa, wb, wc, wo, lng, lnb)


def _topk_rows(s, k):
    R = s.shape[0]
    iota = lax.broadcasted_iota(jnp.int32, s.shape, 0).astype(F32)
    vals, idxs = [], []
    for _ in range(k):
        m = jnp.max(s, axis=0, keepdims=True)
        idx = jnp.min(jnp.where(s == m, iota, float(R)), axis=0, keepdims=True)
        vals.append(m)
        idxs.append(idx)
        s = jnp.where(iota == idx, -jnp.inf, s)
    return jnp.concatenate(vals, axis=0), jnp.concatenate(idxs, axis=0).astype(jnp.int32)


def _pair_candidates(v0, i0, v1, i1, nkeys):
    S = SUBLANES
    vals = [v0[a:a + 1] + v1[:S] for a in range(S)] + [v0[:1] + v1[S:], v0[S:] + v1[:1]]
    eids = [i0[a:a + 1] * nkeys + i1[:S] for a in range(S)] + [i0[:1] * nkeys + i1[S:], i0[S:] * nkeys + i1[:1]]
    cand = jnp.concatenate(vals, axis=0)
    row = lax.broadcasted_iota(jnp.int32, cand.shape, 0)
    slab, b = row >> 3, row & (S - 1)
    flat = jnp.where(slab < S, slab * 16 + b, jnp.where(slab == S, S + b, (S + b) * 16))
    return cand, jnp.concatenate(eids, axis=0), flat.astype(F32)


def _route_kernel(x_ref, sc_ref, sh_ref, wq_ref, sk_ref, h_ref, i_ref, j_ref, g_ref, *, nheads, nkeys, topk):
    h = (x_ref[0] * (1.0 + sc_ref[0]) + sh_ref[0]).astype(BF16)
    h_ref[0] = h
    q = _dot(h, wq_ref[...])
    dk = sk_ref.shape[2]
    i_all, j_all, g_all = [], [], []
    for hd in range(nheads):
        tops = []
        for p in range(2):
            qs = q[:, (2 * hd + p) * dk:(2 * hd + p + 1) * dk].astype(BF16)
            st = _dot_nt(sk_ref[p], qs)
            tops.append(_topk_rows(st, topk))
        (v0, i0), (v1, i1) = tops
        cand, cidx, flat = _pair_candidates(v0, i0, v1, i1, nkeys)
        best, eidx = [], []
        for _ in range(topk):
            m = jnp.max(cand, axis=0, keepdims=True)
            pos = jnp.min(jnp.where(cand == m, flat, float(topk * topk)), axis=0, keepdims=True)
            hit = flat == pos
            best.append(m)
            eidx.append(jnp.sum(jnp.where(hit, cidx, 0), axis=0, keepdims=True))
            cand = jnp.where(hit, -jnp.inf, cand)
        best = jnp.concatenate(best, axis=0)
        e = jnp.concatenate(eidx, axis=0)
        pe = jnp.exp(best - best[0:1])
        g_all.append(pe / jnp.sum(pe, axis=0, keepdims=True))
        i_all.append(e >> (nkeys.bit_length() - 1))
        j_all.append(e & (nkeys - 1))
    i_ref[0] = jnp.concatenate(i_all, axis=0).T
    j_ref[0] = jnp.concatenate(j_all, axis=0).T
    g_ref[0] = jnp.concatenate(g_all, axis=0).T


def peer_route(x, sc, sh, wq, sk, tm):
    B, T, D = x.shape
    mi = _mod_index(sc.shape[0], B)
    npairs = PEER_HEADS * PEER_TOPK
    kern = functools.partial(_route_kernel, nheads=PEER_HEADS, nkeys=sk.shape[1], topk=PEER_TOPK)
    row = lambda w: pl.BlockSpec((1, tm, w), lambda b, t: (b, t, 0))
    return pl.pallas_call(
        kern, grid=(B, T // tm),
        in_specs=[row(D), pl.BlockSpec((1, 1, D), mi), pl.BlockSpec((1, 1, D), mi),
                  pl.BlockSpec(wq.shape, lambda b, t: (0, 0)),
                  pl.BlockSpec(sk.shape, lambda b, t: (0, 0, 0))],
        out_specs=[row(D), row(npairs), row(npairs), row(npairs)],
        out_shape=[jax.ShapeDtypeStruct((B, T, D), BF16),
                   jax.ShapeDtypeStruct((B, T, npairs), jnp.int32),
                   jax.ShapeDtypeStruct((B, T, npairs), jnp.int32),
                   jax.ShapeDtypeStruct((B, T, npairs), F32)],
        compiler_params=_cp(("parallel", "parallel")))(x, sc, sh, wq, sk)


def _peer_act_kernel(h_ref, i_ref, j_ref, g_ref, u_ref, w_ref, a_ref, *, ib, nkeys):
    s_id = pl.program_id(2)

    @pl.when(s_id == 0)
    def _():
        a_ref[...] = jnp.zeros_like(a_ref)

    sc = _dot_nt(h_ref[0], u_ref[...])
    irow = i_ref[0]
    jcol = j_ref[0]
    a = a_ref[...]
    for ii in range(ib):
        got = jnp.take_along_axis(sc[:, ii * nkeys:(ii + 1) * nkeys], jcol, axis=1, mode="promise_in_bounds")
        a = jnp.where(irow == s_id * ib + ii, got, a)
    a_ref[...] = a

    @pl.when(s_id == pl.num_programs(2) - 1)
    def _():
        act = 0.5 * a * (1.0 + lax.erf(a * (2.0 ** -0.5)))
        w_ref[0] = act * g_ref[0]


def peer_act(hb, ii, jj, g, u, tm, ib):
    B, T, D = hb.shape
    npairs = ii.shape[2]
    nkeys = PEER_NKEYS
    steps = u.shape[0] // (ib * nkeys)
    kern = functools.partial(_peer_act_kernel, ib=ib, nkeys=nkeys)
    row = lambda w: pl.BlockSpec((1, tm, w), lambda b, t, s: (b, t, 0))
    return pl.pallas_call(
        kern, grid=(B, T // tm, steps),
        in_specs=[row(D), row(npairs), row(npairs), row(npairs),
                  pl.BlockSpec((ib * nkeys, D), lambda b, t, s: (s, 0))],
        out_specs=row(npairs),
        out_shape=jax.ShapeDtypeStruct((B, T, npairs), F32),
        scratch_shapes=[pltpu.VMEM((tm, npairs), F32)],
        compiler_params=_cp(("parallel", "parallel", "arbitrary")))(hb, ii, jj, g, u)


def _peer_mix_kernel(x_ref, g2_ref, i_ref, j_ref, w_ref, v_ref, lng_ref, lnb_ref, o_ref, p_ref, acc_ref,
                     *, ib, nkeys, pitch):
    s_id = pl.program_id(2)
    tm = x_ref.shape[1]

    @pl.when(s_id == 0)
    def _():
        acc_ref[...] = jnp.zeros_like(acc_ref)
        sub = lax.broadcasted_iota(jnp.int32, (nkeys, i_ref.shape[2]), 0)

        def body(n, carry):
            irow = i_ref[0, pl.ds(n, 1), :]
            jrow = j_ref[0, pl.ds(n, 1), :]
            wrow = w_ref[0, pl.ds(n, 1), :]
            lhs = jnp.where(sub == irow, wrow, 0.0).astype(BF16)
            rhs = jnp.where(sub == jrow, 1.0, 0.0).astype(BF16)
            p_ref[pl.ds(n, nkeys, stride=pitch), :] = _dot_nt(lhs, rhs)
            return carry

        lax.fori_loop(0, tm, body, 0, unroll=MIX_UNROLL)

    parts = [p_ref[pl.ds(pl.multiple_of((s_id * ib + ii) * pitch, 8), tm), :].astype(BF16) for ii in range(ib)]
    acc_ref[...] += _dot(jnp.concatenate(parts, axis=1), v_ref[...])

    @pl.when(s_id == pl.num_programs(2) - 1)
    def _():
        o_ref[0] = _layer_norm(ALPHA * x_ref[0] + g2_ref[0] * acc_ref[...], lng_ref[...], lnb_ref[...])


def peer_mix(x, g2, ii, jj, w, v, lng, lnb, tm, ib):
    B, T, D = x.shape
    npairs = ii.shape[2]
    nkeys = PEER_NKEYS
    steps = v.shape[0] // (ib * nkeys)
    pitch = tm + P_PITCH_PAD
    mi = _mod_index(g2.shape[0], B)
    kern = functools.partial(_peer_mix_kernel, ib=ib, nkeys=nkeys, pitch=pitch)
    row = lambda w_: pl.BlockSpec((1, tm, w_), lambda b, t, s: (b, t, 0))
    c2 = lambda b, t, s: (0, 0)
    return pl.pallas_call(
        kern, grid=(B, T // tm, steps),
        in_specs=[row(D), pl.BlockSpec((1, 1, D), lambda b, t, s: mi(b, t)),
                  row(npairs), row(npairs), row(npairs),
                  pl.BlockSpec((ib * nkeys, D), lambda b, t, s: (s, 0)),
                  pl.BlockSpec(lng.shape, c2), pl.BlockSpec(lnb.shape, c2)],
        out_specs=row(D),
        out_shape=jax.ShapeDtypeStruct((B, T, D), F32),
        scratch_shapes=[pltpu.VMEM((nkeys * pitch, nkeys), F32), pltpu.VMEM((tm, D), F32)],
        compiler_params=_cp(("parallel", "parallel", "arbitrary")))(x, g2, ii, jj, w, v, lng, lnb)


def _filter_features(L):
    t = jnp.linspace(0.0, 1.0, L, dtype=F32)
    w = 2.0 * math.pi * jnp.arange(L, dtype=F32) / L
    bands = jnp.linspace(1e-4, HY_FILTER_BANDS - 1, HY_FILTER_BANDS, dtype=F32)
    z = jnp.concatenate([t[:, None], jnp.cos(w[:, None] * bands[None]), -jnp.sin(w[:, None] * bands[None])], axis=-1)
    z = jnp.pad(z, ((0, 0), (0, LANES - z.shape[1])))
    max_decay = math.log(HY_DECAY_TARGET) / HY_FAST_DECAY
    min_decay = math.log(HY_DECAY_TARGET) / HY_SLOW_DECAY
    deltas = jnp.linspace(min_decay, max_decay, HY_WIDTH, dtype=F32)
    decay = jnp.exp(-t[:, None] * jnp.abs(deltas)[None, :])
    back = lambda a: jnp.concatenate([a[:1], a[:0:-1]], axis=0)
    return z, back(z), decay, back(decay).at[0].set(0.0)


def _hy_filter_kernel(zf_ref, zb_ref, df_ref, db_ref, w1_ref, b1_ref, fr1_ref, w2_ref, b2_ref, fr2_ref,
                      w3f_ref, b3f_ref, w3b_ref, b3b_ref, kf_ref, kb_ref, norm_ref):
    def hidden(z):
        dot = functools.partial(jnp.dot, precision=lax.Precision.HIGHEST, preferred_element_type=F32)
        h = jnp.sin(fr1_ref[...] * (dot(z, w1_ref[...]) + b1_ref[...]))
        return jnp.sin(fr2_ref[...] * (dot(h, w2_ref[...]) + b2_ref[...])), dot

    hf, dot = hidden(zf_ref[...])
    hb, _ = hidden(zb_ref[...])
    kf = (dot(hf, w3f_ref[...]) + b3f_ref[...]) * df_ref[...]
    kb = (dot(hb, w3b_ref[...]) + b3b_ref[...]) * db_ref[...]
    kf_ref[...] = kf
    kb_ref[...] = kb

    @pl.when(pl.program_id(0) == 0)
    def _():
        norm_ref[...] = jnp.zeros_like(norm_ref)

    norm_ref[...] += jnp.sum(jnp.abs(kf) + jnp.abs(kb), axis=0, keepdims=True)


def hy_filter(feats, w1, b1, fr1, w2, b2, fr2, w3, b3):
    zf, zb, df, db = feats
    L, CH = df.shape
    tr = min(1024, L)
    w1p = jnp.pad(w1, ((0, LANES - w1.shape[0]), (0, 0)))
    row = lambda a: pl.BlockSpec((tr, a.shape[1]), lambda i: (i, 0))
    full = lambda a: pl.BlockSpec(a.shape, lambda i: (0, 0))
    args = (zf, zb, df, db, w1p, b1[None, :], fr1[None, :], w2, b2[None, :], fr2[None, :],
            w3[:, :CH], b3[None, :CH], w3[:, CH:], b3[None, CH:])
    return pl.pallas_call(
        _hy_filter_kernel, grid=(L // tr,),
        in_specs=[row(a) for a in args[:4]] + [full(a) for a in args[4:]],
        out_specs=[pl.BlockSpec((tr, CH), lambda i: (i, 0)), pl.BlockSpec((tr, CH), lambda i: (i, 0)),
                   pl.BlockSpec((1, CH), lambda i: (0, 0))],
        out_shape=[jax.ShapeDtypeStruct((L, CH), F32), jax.ShapeDtypeStruct((L, CH), F32),
                   jax.ShapeDtypeStruct((1, CH), F32)],
        compiler_params=_cp(("arbitrary",)))(*args)


def _split_hi_lo(m):
    m32 = jnp.asarray(m, F32)
    hi = m32.astype(BF16)
    return hi, (m32 - hi.astype(F32)).astype(BF16)


def _dot3(m_hi, m_lo, x):
    x_hi = x.astype(BF16)
    x_lo = (x - x_hi.astype(F32)).astype(BF16)
    return _dot(m_hi, x_hi) + _dot(m_hi, x_lo) + _dot(m_lo, x_hi)


def _pad_rows(x, rows):
    if x.shape[0] == rows:
        return x
    return jnp.concatenate([x, jnp.zeros((rows - x.shape[0], x.shape[1]), x.dtype)], axis=0)


def _round_up(n, m):
    return -(-n // m) * m


def _lane_bcast(col):
    return jnp.asarray(np.repeat(np.asarray(col, np.float64).reshape(-1, 1), LANES, axis=1), F32)


def _fft_consts(L):
    N, N2 = 2 * L, HY_N2
    N1 = N // N2
    H1 = N1 // 2
    k1, n1, n2 = np.arange(N1), np.arange(H1), np.arange(N2)
    th = 2 * np.pi * np.outer(k1, n1) / N1
    f1 = np.zeros((2 * N1, _round_up(H1, LANES)))
    f1[:N1, :H1], f1[N1:, :H1] = np.cos(th), -np.sin(th)
    m1 = np.zeros((H1, _round_up(2 * N1, LANES)))
    m1[:, :N1], m1[:, N1:2 * N1] = np.cos(th.T) / N, -np.sin(th.T) / N
    ph = 2 * np.pi * np.outer(n2, n2) / N2
    c, s = np.cos(ph), np.sin(ph)
    m2 = np.block([[c, s], [-s, c]])
    m2i = np.block([[c, -s], [s, c]])
    w = 2 * np.pi / N
    t1 = w * 8 * np.outer(np.arange(N2 // 8), k1)
    t2 = w * np.outer(np.arange(8), k1)
    u1 = w * 8 * np.outer(np.arange(N1 // 8), n2)
    u2 = w * np.outer(np.arange(8), n2)
    tabs = []
    for ang, sign in ((t1, -1.0), (t2, -1.0), (u1, 1.0), (u2, 1.0)):
        tabs += [_lane_bcast(np.cos(ang)), _lane_bcast(sign * np.sin(ang))]
    return dict(N1=N1, mats=_split_hi_lo(f1) + _split_hi_lo(m2) + _split_hi_lo(m2i) + _split_hi_lo(m1), tabs=tuple(tabs))


def _short_conv_block(z, w_ref, b_ref):
    L = z.shape[0]
    row = lax.broadcasted_iota(jnp.int32, z.shape, 0)
    zm = jnp.where(row == 0, 0.0, pltpu.roll(z, 1, axis=0))
    zp = jnp.where(row == L - 1, 0.0, pltpu.roll(z, L - 1, axis=0))
    return zm * w_ref[0:1, :] + z * w_ref[1:2, :] + zp * w_ref[2:3, :] + b_ref[...]


def _cmul(ar, ai, br, bi):
    return ar * br - ai * bi, ar * bi + ai * br


def _fft_stage1(src_ref, as_ref, f1h, f1l, t1r, t1i, t2r, t2i, *, N1, N2):
    H1 = N1 // 2
    pa = 2 * N1 + FFT_PITCH_PAD
    kp = f1h.shape[1]

    C = src_ref.shape[1]

    def body(i, carry):
        n2s = [i * FFT_TIME_GROUP + g for g in range(FFT_TIME_GROUP)]
        xs = jnp.concatenate([src_ref[pl.ds(n2, H1, stride=N2), :] for n2 in n2s], axis=1)
        a = _dot3(f1h[...], f1l[...], _pad_rows(xs, kp))
        for g, n2 in enumerate(n2s):
            ag = a[:, g * C:(g + 1) * C]
            oa = pl.multiple_of((n2 >> 3) * N1, SUBLANES)
            ob = pl.multiple_of((n2 & 7) * N1, SUBLANES)
            tr, ti = _cmul(t1r[pl.ds(oa, N1), :], t1i[pl.ds(oa, N1), :], t2r[pl.ds(ob, N1), :], t2i[pl.ds(ob, N1), :])
            yr, yi = _cmul(ag[:N1], ag[N1:], tr, ti)
            base = pl.multiple_of(n2 * pa, SUBLANES)
            as_ref[pl.ds(base, N1), :] = yr
            as_ref[pl.ds(base + N1, N1), :] = yi
        return carry

    lax.fori_loop(0, N2 // FFT_TIME_GROUP, body, 0, unroll=FFT_UNROLL)


def _fft_stage2(as_ref, k1s, m2h, m2l, *, N1, N2):
    pa = 2 * N1 + FFT_PITCH_PAD
    v = jnp.concatenate(
        [jnp.concatenate([as_ref[pl.ds(k1, N2, stride=pa), :], as_ref[pl.ds(N1 + k1, N2, stride=pa), :]], axis=0)
         for k1 in k1s], axis=1)
    return _dot3(m2h[...], m2l[...], v)


def _hy_spec_kernel(kf_ref, kb_ref, norm_ref, f1h, f1l, m2h, m2l, t1r, t1i, t2r, t2i, o_ref, as_ref, *, N1, N2):
    C = kf_ref.shape[1]
    inv = 1.0 / norm_ref[...]
    for part, src in enumerate((kf_ref, kb_ref)):
        _fft_stage1(src, as_ref, f1h, f1l, t1r, t1i, t2r, t2i, N1=N1, N2=N2)

        def body(i, carry):
            k1s = [i * FFT_FREQ_GROUP + g for g in range(FFT_FREQ_GROUP)]
            x = _fft_stage2(as_ref, k1s, m2h, m2l, N1=N1, N2=N2)
            for g, k1 in enumerate(k1s):
                xg = x[:, g * C:(g + 1) * C] * inv
                if part == 0:
                    o_ref[k1] = xg
                else:
                    o_ref[k1] = o_ref[k1] + (1.0 - 2.0 * (k1 & 1)) * xg
            return carry

        lax.fori_loop(0, N1 // FFT_FREQ_GROUP, body, 0)


def hy_spectrum(kf, kb, norm, fc):
    L, CH = kf.shape
    N1, N2, C = fc["N1"], HY_N2, HY_CB
    f1h, f1l, m2h, m2l = fc["mats"][:4]
    tabs = fc["tabs"][:4]
    kern = functools.partial(_hy_spec_kernel, N1=N1, N2=N2)
    const = lambda a: pl.BlockSpec(a.shape, lambda j: (0, 0))
    return pl.pallas_call(
        kern, grid=(CH // C,),
        in_specs=[pl.BlockSpec((L, C), lambda j: (0, j)), pl.BlockSpec((L, C), lambda j: (0, j)),
                  pl.BlockSpec((1, C), lambda j: (0, j))]
        + [const(a) for a in (f1h, f1l, m2h, m2l) + tuple(tabs)],
        out_specs=pl.BlockSpec((N1, 2 * N2, C), lambda j: (0, 0, j)),
        out_shape=jax.ShapeDtypeStruct((N1, 2 * N2, CH), F32),
        scratch_shapes=[pltpu.VMEM((N2 * (2 * N1 + FFT_PITCH_PAD), C), F32)],
        compiler_params=_cp(("parallel",)))(kf, kb, norm, f1h, f1l, m2h, m2l, *tabs)


def _hy_conv_kernel(x0_ref, x1_ref, v_ref, w0_ref, w1_ref, wv_ref, b0_ref, b1_ref, bv_ref, hb_ref, ksp_ref,
                    f1h, f1l, m2h, m2l, m2ih, m2il, m1h, m1l, t1r, t1i, t2r, t2i, u1r, u1i, u2r, u2i,
                    o_ref, uv_ref, g_ref, as_ref, bs_ref, *, N1, N2):
    H1 = N1 // 2
    pb = 2 * N2 + FFT_PITCH_PAD
    uv_ref[...] = _short_conv_block(v_ref[0], wv_ref, bv_ref) * _short_conv_block(x1_ref[0], w1_ref, b1_ref)
    g_ref[...] = _short_conv_block(x0_ref[0], w0_ref, b0_ref)
    _fft_stage1(uv_ref, as_ref, f1h, f1l, t1r, t1i, t2r, t2i, N1=N1, N2=N2)

    C = uv_ref.shape[1]

    def freq_body(i, carry):
        k1s = [i * FFT_FREQ_GROUP + g for g in range(FFT_FREQ_GROUP)]
        x = _fft_stage2(as_ref, k1s, m2h, m2l, N1=N1, N2=N2)
        ks = jnp.concatenate([ksp_ref[k1] for k1 in k1s], axis=1)
        yr, yi = _cmul(x[:N2], x[N2:], ks[:N2], ks[N2:])
        b = _dot3(m2ih[...], m2il[...], jnp.concatenate([yr, yi], axis=0))
        for g, k1 in enumerate(k1s):
            bg = b[:, g * C:(g + 1) * C]
            oa = pl.multiple_of((k1 >> 3) * N2, SUBLANES)
            ob = pl.multiple_of((k1 & 7) * N2, SUBLANES)
            ur, ui = _cmul(u1r[pl.ds(oa, N2), :], u1i[pl.ds(oa, N2), :], u2r[pl.ds(ob, N2), :], u2i[pl.ds(ob, N2), :])
            br, bi = _cmul(bg[:N2], bg[N2:], ur, ui)
            base = pl.multiple_of(k1 * pb, SUBLANES)
            bs_ref[pl.ds(base, N2), :] = br
            bs_ref[pl.ds(base + N2, N2), :] = bi
        return carry

    lax.fori_loop(0, N1 // FFT_FREQ_GROUP, freq_body, 0, unroll=FFT_UNROLL)
    kp = m1h.shape[1]

    def time_body(i, carry):
        n2s = [i * FFT_TIME_GROUP + g for g in range(FFT_TIME_GROUP)]
        w = jnp.concatenate(
            [_pad_rows(jnp.concatenate([bs_ref[pl.ds(n2, N1, stride=pb), :], bs_ref[pl.ds(N2 + n2, N1, stride=pb), :]],
                                       axis=0), kp) for n2 in n2s], axis=1)
        y = _dot3(m1h[...], m1l[...], w)
        for g, n2 in enumerate(n2s):
            rows = pl.ds(n2, H1, stride=N2)
            o_ref[0, rows, :] = (y[:, g * C:(g + 1) * C] + uv_ref[rows, :] * hb_ref[...]) * g_ref[rows, :]
        return carry

    lax.fori_loop(0, N2 // FFT_TIME_GROUP, time_body, 0, unroll=FFT_UNROLL)


def hy_conv(zh, short_w, short_b, hy_bias, ksp, fc):
    B, L, W3 = zh.shape
    CH = W3 // 3
    N1, N2, C = fc["N1"], HY_N2, HY_CB
    nb = CH // C
    kern = functools.partial(_hy_conv_kernel, N1=N1, N2=N2)
    sync = pl.Buffered(1)
    zspec = lambda g: pl.BlockSpec((1, L, C), lambda j, b: (b, 0, g * nb + j), pipeline_mode=sync)
    wspec = lambda g: pl.BlockSpec((HY_SHORT, C), lambda j, b: (0, g * nb + j))
    bspec = lambda g: pl.BlockSpec((1, C), lambda j, b: (0, g * nb + j))
    const = lambda a: pl.BlockSpec(a.shape, lambda j, b: (0, 0), pipeline_mode=sync)
    consts = fc["mats"] + fc["tabs"]
    return pl.pallas_call(
        kern, grid=(nb, B),
        in_specs=[zspec(0), zspec(1), zspec(2), wspec(0), wspec(1), wspec(2), bspec(0), bspec(1), bspec(2),
                  pl.BlockSpec((1, C), lambda j, b: (0, j)),
                  pl.BlockSpec((N1, 2 * N2, C), lambda j, b: (0, 0, j), pipeline_mode=sync)]
        + [const(a) for a in consts],
        out_specs=pl.BlockSpec((1, L, C), lambda j, b: (b, 0, j)),
        out_shape=jax.ShapeDtypeStruct((B, L, CH), F32),
        scratch_shapes=[pltpu.VMEM((L, C), F32), pltpu.VMEM((L, C), F32),
                        pltpu.VMEM((N2 * (2 * N1 + FFT_PITCH_PAD), C), F32),
                        pltpu.VMEM((N1 * (2 * N2 + FFT_PITCH_PAD), C), F32)],
        compiler_params=_cp(("parallel", "parallel")))(
            zh, zh, zh, short_w, short_w, short_w, short_b, short_b, short_b, hy_bias, ksp, *consts)


def _dft_consts(L):
    N = 2 * L
    k = np.arange(N)
    ang = 2 * np.pi * np.outer(k, k) / N
    fwd = np.concatenate([np.cos(ang), -np.sin(ang)], axis=0)
    inv = np.concatenate([np.cos(ang[:L]), -np.sin(ang[:L])], axis=1) / N
    return _split_hi_lo(fwd[:, :L]) + _split_hi_lo(fwd) + _split_hi_lo(inv)


def _hy_small_kernel(x0_ref, x1_ref, v_ref, w0_ref, w1_ref, wv_ref, b0_ref, b1_ref, bv_ref, hb_ref,
                     kf_ref, kb_ref, norm_ref, fh, fl, ffh, ffl, ih, il, o_ref):
    N = 2 * kf_ref.shape[0]
    uv = _short_conv_block(v_ref[0], wv_ref, bv_ref) * _short_conv_block(x1_ref[0], w1_ref, b1_ref)
    g = _short_conv_block(x0_ref[0], w0_ref, b0_ref)
    x = _dot3(fh[...], fl[...], uv)
    kfull = jnp.concatenate([kf_ref[...], kb_ref[...]], axis=0) * (1.0 / norm_ref[...])
    ks = _dot3(ffh[...], ffl[...], kfull)
    yr, yi = _cmul(x[:N], x[N:], ks[:N], ks[N:])
    y = _dot3(ih[...], il[...], jnp.concatenate([yr, yi], axis=0))
    o_ref[0] = (y + uv * hb_ref[...]) * g


def hy_conv_small(zh, short_w, short_b, hy_bias, kf, kb, norm, dc):
    B, L, W3 = zh.shape
    CH = W3 // 3
    C = HY_CB
    nb = CH // C
    zspec = lambda g: pl.BlockSpec((1, L, C), lambda j, b: (b, 0, g * nb + j))
    wspec = lambda g: pl.BlockSpec((HY_SHORT, C), lambda j, b: (0, g * nb + j))
    bspec = lambda g: pl.BlockSpec((1, C), lambda j, b: (0, g * nb + j))
    const = lambda a: pl.BlockSpec(a.shape, lambda j, b: (0, 0))
    return pl.pallas_call(
        _hy_small_kernel, grid=(nb, B),
        in_specs=[zspec(0), zspec(1), zspec(2), wspec(0), wspec(1), wspec(2), bspec(0), bspec(1), bspec(2),
                  pl.BlockSpec((1, C), lambda j, b: (0, j)),
                  pl.BlockSpec((L, C), lambda j, b: (0, j)), pl.BlockSpec((L, C), lambda j, b: (0, j)),
                  pl.BlockSpec((1, C), lambda j, b: (0, j))]
        + [const(a) for a in dc],
        out_specs=pl.BlockSpec((1, L, C), lambda j, b: (b, 0, j)),
        out_shape=jax.ShapeDtypeStruct((B, L, CH), F32),
        compiler_params=_cp(("parallel", "parallel")))(
            zh, zh, zh, short_w, short_w, short_w, short_b, short_b, short_b, hy_bias, kf, kb, norm, *dc)


def _gqa_slot_offsets():
    group = GQA_HEADS // GQA_KV_HEADS
    return [(h * LANES + (h // group) * HEAD_DIM) for h in range(GQA_HEADS)]


def _pad_gqa_cols(w):
    out = jnp.zeros((w.shape[0], GQA_HEADS * LANES), w.dtype)
    for h, off in enumerate(_gqa_slot_offsets()):
        out = out.at[:, off:off + HEAD_DIM].set(w[:, h * HEAD_DIM:(h + 1) * HEAD_DIM])
    return out


def _pad_gqa_vec(g):
    out = jnp.zeros((GQA_HEADS * LANES,), g.dtype)
    for off in _gqa_slot_offsets():
        out = out.at[off:off + HEAD_DIM].set(g)
    return out


def _rope_consts(T):
    half = HEAD_DIM // 2
    quarter = half // 2
    t = jnp.arange(T)
    inv = ROPE_THETA ** (-jnp.arange(0, half, 2, dtype=F32) / half)
    ar = (t // GRID_W).astype(F32)[:, None] * inv[None, :]
    ac = (t % GRID_W).astype(F32)[:, None] * inv[None, :]
    ang = jnp.concatenate([ar, ar, ac, ac] * (LANES // HEAD_DIM), axis=1)
    P = np.zeros((LANES, LANES), np.float32)
    for o in range(0, LANES, half):
        for j in range(quarter):
            P[o + j + quarter, o + j] = -1.0
            P[o + j, o + j + quarter] = 1.0
    return jnp.cos(ang), jnp.sin(ang), jnp.asarray(P, BF16)


def _head_mean_mat():
    m = np.zeros((LANES, LANES), np.float32)
    for s in range(0, LANES, HEAD_DIM):
        m[s:s + HEAD_DIM, s:s + HEAD_DIM] = 1.0 / HEAD_DIM
    return jnp.asarray(m, BF16)


def _na_bias_table(rpb, rows):
    wr = min(NA_WIN_ROWS, rows)
    wc = NA_WIN_COLS
    q = np.arange(GRID_W)
    c0 = np.clip(q - wc // 2, 0, GRID_W - wc)
    kc = np.arange(GRID_W)
    valid = (kc[None, :] >= c0[:, None]) & (kc[None, :] < c0[:, None] + wc)
    dc = np.clip(kc[None, :] - q[:, None] + (NA_WIN_COLS - 1), 0, 2 * NA_WIN_COLS - 2)
    assert wr == NA_WIN_ROWS
    H, ncols = rpb.shape[0], rpb.shape[2]
    onehot = (dc[None] == np.arange(ncols)[:, None, None]) & valid[None]
    tq = jnp.einsum('hdc,cqk->hdqk', rpb, jnp.asarray(onehot, F32), precision=lax.Precision.HIGHEST)
    tq = jnp.where(jnp.asarray(valid)[None, None], tq, NEG)
    slabs = [jnp.transpose(tq[:, o:o + wr], (0, 2, 1, 3)).reshape(H, GRID_W, wr * GRID_W) for o in range(wr)]
    return jnp.stack(slabs, axis=0)


def kernel(x, c, ctx, c_ctx, w_ada, b_ada, w_in, na_rpb, gqa_q_gain, gqa_k_gain, hy_short_w, hy_short_b,
           hy_f_w1, hy_f_b1, hy_f_freq1, hy_f_w2, hy_f_b2, hy_f_freq2, hy_f_w3, hy_f_b3, hy_bias,
           w_br_a, w_br_b, w_br_c, w_out, ln1_g, ln1_b, peer_w_q, peer_sub_keys, peer_u, peer_v, ln2_g, ln2_b):
    B, T, D = x.shape
    C = ctx.shape[1]
    depth = w_ada.shape[0]
    rows = T // GRID_W
    wr = min(NA_WIN_ROWS, rows)
    NAW = NA_HEADS * HEAD_DIM
    GQW = GQA_HEADS * HEAD_DIM
    GKW = GQA_KV_HEADS * HEAD_DIM
    o_na, o_gq, o_gk, o_gv, o_hy, o_gl = 0, 3 * NAW, 3 * NAW + GQW, 3 * NAW + GQW + GKW, 3 * NAW + GQW + 2 * GKW, \
        3 * NAW + GQW + 2 * GKW + 3 * HY_WIDTH

    cos_t, sin_t, rope_perm = _rope_consts(T)
    head_mean = _head_mean_mat()
    fft_c = _fft_consts(T)
    feats_t = _filter_features(T)
    feats_c = _filter_features(C)
    dft_c = _dft_consts(C)
    c_all = jnp.concatenate([c, c_ctx[None, :], jnp.zeros((16 - B - 1, D), F32)], axis=0)
    tm = min(512, T)
    tmc = C

    for l in range(depth):
        need_ctx = l < depth - 1
        mod = ada_mod(c_all, w_ada[l], b_ada[l][None, :])
        ml = mod[:B].reshape(B, 1, 6, D)
        mc = mod[B:B + 1].reshape(1, 1, 6, D)
        sh1, sc1, g1, sh2, sc2, g2 = [ml[:, :, i] for i in range(6)]
        csh1, csc1, cg1, csh2, csc2, cg2 = [mc[:, :, i] for i in range(6)]

        wl = w_in[l]
        w_na = wl[:, o_na:o_gq].astype(BF16)
        w_gq = jnp.concatenate([_pad_gqa_cols(wl[:, o_gq:o_gk]), wl[:, o_gk:o_hy]], axis=1).astype(BF16)
        w_hy = wl[:, o_hy:o_gl].astype(BF16)
        w_gl = wl[:, o_gl:].astype(BF16)
        gqk = jnp.concatenate([_pad_gqa_vec(gqa_q_gain[l]), jnp.tile(gqa_k_gain[l], GQA_KV_HEADS)])[None, :]
        wa = w_br_a[l].astype(BF16)
        wb = _pad_gqa_cols(w_br_b[l].T).T.astype(BF16)
        wc = w_br_c[l].astype(BF16)
        wo = w_out[l].astype(BF16)
        lng1, lnb1 = ln1_g[l][None, :], ln1_b[l][None, :]
        lng2, lnb2 = ln2_g[l][None, :], ln2_b[l][None, :]
        wq = peer_w_q[l].astype(BF16)
        sk = peer_sub_keys[l].astype(BF16)
        ub = peer_u[l].astype(BF16)
        vb = peer_v[l].astype(BF16)
        bias_tab = _na_bias_table(na_rpb[l], rows)

        qkv_a = mod_matmul(x, sc1, sh1, w_na, BF16, tm)
        qkv_ac = mod_matmul(ctx, csc1, csh1, w_na, BF16, tmc)
        ya = na_attention(qkv_a, qkv_ac, bias_tab, GRID_W, wr)

        qb, kb, vb_ = gqa_proj(x, sc1, sh1, w_gq, gqk, head_mean, rope_perm, cos_t, sin_t, True, tm)
        qbc, kbc, vbc = gqa_proj(ctx, csc1, csh1, w_gq, gqk, head_mean, rope_perm, cos_t[:C], sin_t[:C], False, tmc)
        yb = gqa_attention(qb, jnp.concatenate([kb, kbc], axis=1), jnp.concatenate([vb_, vbc], axis=1), min(256, T))

        zh = mod_matmul(x, sc1, sh1, w_hy, F32, tm)
        filt = functools.partial(hy_filter, w1=hy_f_w1[l], b1=hy_f_b1[l], fr1=hy_f_freq1[l], w2=hy_f_w2[l],
                                 b2=hy_f_b2[l], fr2=hy_f_freq2[l], w3=hy_f_w3[l], b3=hy_f_b3[l])
        ksp = hy_spectrum(*filt(feats_t), fft_c)
        yc = hy_conv(zh, hy_short_w[l], hy_short_b[l][None, :], hy_bias[l][None, :], ksp, fft_c)

        x_mid = merge_block(x, sc1, sh1, g1, ya, yb, yc, w_gl, wa, wb, wc, wo, lng1, lnb1, min(256, T))

        hb, ii, jj, gg = peer_route(x_mid, sc2, sh2, wq, sk, min(256, T))
        wgt = peer_act(hb, ii, jj, gg, ub, min(1024, T), 8)
        x_new = peer_mix(x_mid, g2, ii, jj, wgt, vb, lng2, lnb2, min(256, T), 32)

        if need_ctx:
            yac = ctx_pair_attention(qkv_ac)
            ybc = gqa_attention(qbc, kbc, vbc, C)
            zhc = mod_matmul(ctx, csc1, csh1, w_hy, F32, tmc)
            ycc = hy_conv_small(zhc, hy_short_w[l], hy_short_b[l][None, :], hy_bias[l][None, :],
                                *filt(feats_c), dft_c)
            c_mid = merge_block(ctx, csc1, csh1, cg1, yac, ybc, ycc, w_gl, wa, wb, wc, wo, lng1, lnb1, tmc)
            hbc, iic, jjc, ggc = peer_route(c_mid, csc2, csh2, wq, sk, tmc)
            wgtc = peer_act(hbc, iic, jjc, ggc, ub, tmc, 8)
            ctx = peer_mix(c_mid, cg2, iic, jjc, wgtc, vb, lng2, lnb2, tmc, 32)
        x = x_new
    return x
```

```python
import functools
import math

import numpy as np
import jax
import jax.numpy as jnp
from jax import lax
from jax.experimental import pallas as pl
from jax.experimental.pallas import tpu as pltpu

F32 = jnp.float32
BF16 = jnp.bfloat16

GRID_W = 64
NA_HEADS = 8
NA_WIN_ROWS = 8
NA_WIN_COLS = 16
GQA_HEADS = 8
GQA_KV_HEADS = 2
HEAD_DIM = 64
ROPE_THETA = 10000.0
HY_WIDTH = 512
HY_SHORT = 3
HY_FILTER_BANDS = 8
HY_FAST_DECAY = 0.3
HY_SLOW_DECAY = 1.5
HY_DECAY_TARGET = 1e-2
PEER_HEADS = 8
PEER_NKEYS = 128
PEER_TOPK = 16
DEPTH = 4
ALPHA = (2 * DEPTH) ** 0.25
LN_EPS = 1e-5
RMS_EPS = 1e-6

LANES = 128
SUBLANES = 8
HY_N2 = 128
HY_CB = LANES
FFT_PITCH_PAD = 8
FFT_TIME_GROUP = 4
FFT_FREQ_GROUP = 2
ACT_KEY_ROWS = 8
MIX_KEY_ROWS = 32
NA_ROWS_PER_STEP = 2
FFT_UNROLL = 8
MIX_UNROLL = 64
VMEM_LIMIT = 56 * 1024 * 1024
P_PITCH_PAD = 8
NEG = -1e30


def _cp(sem, vmem=VMEM_LIMIT):
    return pltpu.CompilerParams(dimension_semantics=sem, vmem_limit_bytes=vmem)


def _dot(a, b):
    return jnp.dot(a, b, preferred_element_type=F32)


def _dot_nt(a, b):
    return lax.dot_general(a, b, (((1,), (1,)), ((), ())), preferred_element_type=F32)


def _dot_split(a, b):
    a0 = a.astype(BF16)
    r1 = a - a0.astype(F32)
    a1 = r1.astype(BF16)
    a2 = (r1 - a1.astype(F32)).astype(BF16)
    return _dot(a0, b) + _dot(a1, b) + _dot(a2, b)


def _ada_kernel(c_ref, w_ref, b_ref, o_ref):
    c = c_ref[...]
    s = c * jax.nn.sigmoid(c)
    o_ref[...] = jnp.dot(s, w_ref[...], preferred_element_type=F32,
                         precision=lax.Precision.HIGHEST) + b_ref[...]


def ada_mod(c_all, w, b):
    R, D = c_all.shape
    N = w.shape[1]
    tn = 1024 if N % 1024 == 0 else N
    return pl.pallas_call(
        _ada_kernel, grid=(N // tn,),
        in_specs=[pl.BlockSpec((R, D), lambda j: (0, 0)),
                  pl.BlockSpec((D, tn), lambda j: (0, j)),
                  pl.BlockSpec((1, tn), lambda j: (0, j))],
        out_specs=pl.BlockSpec((R, tn), lambda j: (0, j)),
        out_shape=jax.ShapeDtypeStruct((R, N), F32),
        compiler_params=_cp(("arbitrary",)))(c_all, w, b)


def _modmm_kernel(x_ref, sc_ref, sh_ref, w_ref, o_ref):
    h = x_ref[0] * (1.0 + sc_ref[0]) + sh_ref[0]
    o_ref[0] = _dot(h.astype(BF16), w_ref[...]).astype(o_ref.dtype)


def _mod_index(Bm, B):
    if Bm == B:
        return lambda b, t: (b, 0, 0)
    return lambda b, t: (0, 0, 0)


def mod_matmul(x, sc, sh, w, out_dtype, tm):
    B, T, D = x.shape
    N = w.shape[1]
    mi = _mod_index(sc.shape[0], B)
    return pl.pallas_call(
        _modmm_kernel, grid=(B, T // tm),
        in_specs=[pl.BlockSpec((1, tm, D), lambda b, t: (b, t, 0)),
                  pl.BlockSpec((1, 1, D), mi), pl.BlockSpec((1, 1, D), mi),
                  pl.BlockSpec((D, N), lambda b, t: (0, 0))],
        out_specs=pl.BlockSpec((1, tm, N), lambda b, t: (b, t, 0)),
        out_shape=jax.ShapeDtypeStruct((B, T, N), out_dtype),
        compiler_params=_cp(("parallel", "parallel")))(x, sc, sh, w)


def _gqa_proj_kernel(x_ref, sc_ref, sh_ref, w_ref, gain_ref, gm_ref, perm_ref, cos_ref, sin_ref,
                     q_ref, k_ref, v_ref, *, rope, qw):
    h = x_ref[0] * (1.0 + sc_ref[0]) + sh_ref[0]
    z = _dot(h.astype(BF16), w_ref[...])
    outs = []
    for s in range(qw // LANES + 1):
        sl = slice(s * LANES, (s + 1) * LANES)
        y = z[:, sl]
        y = y * lax.rsqrt(_dot_split(y * y, gm_ref[...]) + RMS_EPS) * gain_ref[:, sl]
        if rope:
            y = y * cos_ref[...] + _dot_split(y, perm_ref[...]) * sin_ref[...]
        outs.append(y)
    q_ref[0] = jnp.concatenate(outs[:-1], axis=1).astype(q_ref.dtype)
    k_ref[0] = outs[-1].astype(k_ref.dtype)
    v_ref[0] = z[:, qw + LANES:].astype(v_ref.dtype)


def gqa_proj(x, sc, sh, w, gain, gm, perm, cos, sin, rope, tm):
    B, T, D = x.shape
    N = w.shape[1]
    qw = N - 2 * LANES
    mi = _mod_index(sc.shape[0], B)
    c2 = lambda b, t: (0, 0)
    kern = functools.partial(_gqa_proj_kernel, rope=rope, qw=qw)
    return pl.pallas_call(
        kern, grid=(B, T // tm),
        in_specs=[pl.BlockSpec((1, tm, D), lambda b, t: (b, t, 0)),
                  pl.BlockSpec((1, 1, D), mi), pl.BlockSpec((1, 1, D), mi),
                  pl.BlockSpec((D, N), c2),
                  pl.BlockSpec((1, qw + LANES), c2),
                  pl.BlockSpec((LANES, LANES), c2), pl.BlockSpec((LANES, LANES), c2),
                  pl.BlockSpec((tm, LANES), lambda b, t: (t, 0)),
                  pl.BlockSpec((tm, LANES), lambda b, t: (t, 0))],
        out_specs=[pl.BlockSpec((1, tm, qw), lambda b, t: (b, t, 0)),
                   pl.BlockSpec((1, tm, LANES), lambda b, t: (b, t, 0)),
                   pl.BlockSpec((1, tm, LANES), lambda b, t: (b, t, 0))],
        out_shape=[jax.ShapeDtypeStruct((B, T, qw), BF16),
                   jax.ShapeDtypeStruct((B, T, LANES), BF16),
                   jax.ShapeDtypeStruct((B, T, LANES), BF16)],
        compiler_params=_cp(("parallel", "parallel")))(x, sc, sh, w, gain, gm, perm, cos, sin)


def _gqa_attn_kernel(q_ref, k_ref, v_ref, o_ref, *, nheads, scale):
    k = k_ref[0]
    v = v_ref[0]
    for h in range(nheads):
        sl = slice(h * LANES, (h + 1) * LANES)
        s = _dot_nt(q_ref[0, :, sl], k) * scale
        m = jnp.max(s, axis=-1, keepdims=True)
        p = jnp.exp(s - m)
        den = jnp.sum(p, axis=-1, keepdims=True)
        o = _dot(p.astype(BF16), v)
        o_ref[0, :, sl] = (o / den).astype(o_ref.dtype)


def gqa_attention(q, k, v, tq):
    B, T, QW = q.shape
    Lk = k.shape[1]
    kern = functools.partial(_gqa_attn_kernel, nheads=QW // LANES, scale=HEAD_DIM ** -0.5)
    return pl.pallas_call(
        kern, grid=(B, T // tq),
        in_specs=[pl.BlockSpec((1, tq, QW), lambda b, t: (b, t, 0)),
                  pl.BlockSpec((1, Lk, LANES), lambda b, t: (b, 0, 0)),
                  pl.BlockSpec((1, Lk, LANES), lambda b, t: (b, 0, 0))],
        out_specs=pl.BlockSpec((1, tq, QW), lambda b, t: (b, t, 0)),
        out_shape=jax.ShapeDtypeStruct((B, T, QW), BF16),
        compiler_params=_cp(("parallel", "parallel")))(q, k, v)


def _pair_attention(q2, k_loc, v_loc, bias_pair, k_ctx, v_ctx, scale):
    Lq = q2.shape[0]
    low = lax.broadcasted_iota(jnp.int32, q2.shape, 1) < HEAD_DIM
    zero = jnp.zeros_like(q2)
    qs = jnp.concatenate([jnp.where(low, q2, zero), jnp.where(low, zero, q2)], axis=0)
    s_ctx = _dot_nt(qs, k_ctx) * scale
    mx = jnp.max(s_ctx, axis=-1, keepdims=True)
    if k_loc is not None:
        s_loc = _dot_nt(qs, k_loc) * scale + jnp.concatenate(bias_pair, axis=0)
        mx = jnp.maximum(mx, jnp.max(s_loc, axis=-1, keepdims=True))
        p_loc = jnp.exp(s_loc - mx)
    p_ctx = jnp.exp(s_ctx - mx)
    den = jnp.sum(p_ctx, axis=-1, keepdims=True)
    o = _dot(p_ctx.astype(BF16), v_ctx)
    if k_loc is not None:
        den = den + jnp.sum(p_loc, axis=-1, keepdims=True)
        o = o + _dot(p_loc.astype(BF16), v_loc)
    o = o / den
    return jnp.where(low, o[:Lq], o[Lq:])


def _na_kernel(q_ref, k_ref, v_ref, kc_ref, vc_ref, *rest, rows, wr, width, scale):
    bt_refs, o_ref = rest[:-1], rest[-1]
    nk = wr * width
    for rr, bt_ref in enumerate(bt_refs):
        r = pl.program_id(1) * len(bt_refs) + rr
        r0 = jnp.clip(r - wr // 2, 0, rows - wr)
        start = pl.multiple_of(r0 * width, width)
        qrows = slice(rr * width, (rr + 1) * width)
        for pr in range(q_ref.shape[2] // LANES):
            sl = slice(pr * LANES, (pr + 1) * LANES)
            o = _pair_attention(q_ref[0, qrows, sl],
                                k_ref[0, pl.ds(start, nk), sl], v_ref[0, pl.ds(start, nk), sl],
                                (bt_ref[0, 2 * pr], bt_ref[0, 2 * pr + 1]),
                                kc_ref[0, :, sl], vc_ref[0, :, sl], scale)
            o_ref[0, qrows, sl] = o.astype(o_ref.dtype)


def na_attention(qkv, qkv_c, bias_tab, width, wr):
    B, T, W3 = qkv.shape
    HW = W3 // 3
    C = qkv_c.shape[1]
    rows = T // width
    nr = NA_ROWS_PER_STEP
    kern = functools.partial(_na_kernel, rows=rows, wr=wr, width=width, scale=HEAD_DIM ** -0.5)

    def bt_spec(rr):
        def index(b, i):
            r = i * nr + rr
            r0 = jnp.clip(r - wr // 2, 0, rows - wr)
            return (r0 - r + wr - 1, 0, 0, 0)
        return pl.BlockSpec((1,) + bias_tab.shape[1:], index)

    return pl.pallas_call(
        kern, grid=(B, rows // nr),
        in_specs=[pl.BlockSpec((1, nr * width, HW), lambda b, i: (b, i, 0)),
                  pl.BlockSpec((1, T, HW), lambda b, i: (b, 0, 1)),
                  pl.BlockSpec((1, T, HW), lambda b, i: (b, 0, 2)),
                  pl.BlockSpec((1, C, HW), lambda b, i: (b, 0, 1)),
                  pl.BlockSpec((1, C, HW), lambda b, i: (b, 0, 2))] + [bt_spec(rr) for rr in range(nr)],
        out_specs=pl.BlockSpec((1, nr * width, HW), lambda b, i: (b, i, 0)),
        out_shape=jax.ShapeDtypeStruct((B, T, HW), BF16),
        compiler_params=_cp(("parallel", "arbitrary")))(qkv, qkv, qkv, qkv_c, qkv_c, *([bias_tab] * nr))


def _ctx_pair_kernel(q_ref, k_ref, v_ref, o_ref, *, scale):
    for pr in range(q_ref.shape[2] // LANES):
        sl = slice(pr * LANES, (pr + 1) * LANES)
        o = _pair_attention(q_ref[0, :, sl], None, None, None, k_ref[0, :, sl], v_ref[0, :, sl], scale)
        o_ref[0, :, sl] = o.astype(o_ref.dtype)


def ctx_pair_attention(qkv_c):
    B, C, W3 = qkv_c.shape
    HW = W3 // 3
    kern = functools.partial(_ctx_pair_kernel, scale=HEAD_DIM ** -0.5)
    return pl.pallas_call(
        kern, grid=(B,),
        in_specs=[pl.BlockSpec((1, C, HW), lambda b: (b, 0, 0)),
                  pl.BlockSpec((1, C, HW), lambda b: (b, 0, 1)),
                  pl.BlockSpec((1, C, HW), lambda b: (b, 0, 2))],
        out_specs=pl.BlockSpec((1, C, HW), lambda b: (b, 0, 0)),
        out_shape=jax.ShapeDtypeStruct((B, C, HW), BF16),
        compiler_params=_cp(("parallel",)))(qkv_c, qkv_c, qkv_c)


def _layer_norm(z, g, b):
    mu = jnp.mean(z, axis=-1, keepdims=True)
    zc = z - mu
    var = jnp.mean(zc * zc, axis=-1, keepdims=True)
    return zc * lax.rsqrt(var + LN_EPS) * g + b


def _merge_kernel(x_ref, sc_ref, sh_ref, g1_ref, ya_ref, yb_ref, yc_ref, wg_ref, wa_ref, wb_ref, wc_ref,
                  wo_ref, lng_ref, lnb_ref, o_ref):
    x = x_ref[0]
    D = x.shape[1]
    h = x * (1.0 + sc_ref[0]) + sh_ref[0]
    gl = jax.nn.sigmoid(_dot(h.astype(BF16), wg_ref[...]))
    m = (gl[:, :D] * _dot(ya_ref[0], wa_ref[...])
         + gl[:, D:2 * D] * _dot(yb_ref[0], wb_ref[...])
         + gl[:, 2 * D:] * _dot(yc_ref[0].astype(BF16), wc_ref[...]))
    y = _dot(m.astype(BF16), wo_ref[...])
    o_ref[0] = _layer_norm(ALPHA * x + g1_ref[0] * y, lng_ref[...], lnb_ref[...])


def merge_block(x, sc, sh, g1, ya, yb, yc, wg, wa, wb, wc, wo, lng, lnb, tm):
    B, T, D = x.shape
    mi = _mod_index(sc.shape[0], B)
    c2 = lambda b, t: (0, 0)
    row = lambda w: pl.BlockSpec((1, tm, w), lambda b, t: (b, t, 0))
    full = lambda a: pl.BlockSpec(a.shape, c2)
    return pl.pallas_call(
        _merge_kernel, grid=(B, T // tm),
        in_specs=[row(D), pl.BlockSpec((1, 1, D), mi), pl.BlockSpec((1, 1, D), mi), pl.BlockSpec((1, 1, D), mi),
                  row(ya.shape[2]), row(yb.shape[2]), row(yc.shape[2]),
                  full(wg), full(wa), full(wb), full(wc), full(wo), full(lng), full(lnb)],
        out_specs=row(D),
        out_shape=jax.ShapeDtypeStruct((B, T, D), F32),
        compiler_params=_cp(("parallel", "parallel")))(x, sc, sh, g1, ya, yb, yc, wg, wa, wb, wc, wo, lng, lnb)


def _topk_rows(s, k):
    R = s.shape[0]
    iota = lax.broadcasted_iota(jnp.int32, s.shape, 0).astype(F32)
    vals, idxs = [], []
    for _ in range(k):
        m = jnp.max(s, axis=0, keepdims=True)
        idx = jnp.min(jnp.where(s == m, iota, float(R)), axis=0, keepdims=True)
        vals.append(m)
        idxs.append(idx)
        s = jnp.where(iota == idx, -jnp.inf, s)
    return jnp.concatenate(vals, axis=0), jnp.concatenate(idxs, axis=0).astype(jnp.int32)


def _pair_candidates(v0, i0, v1, i1, nkeys):
    S = SUBLANES
    vals = [v0[a:a + 1] + v1[:S] for a in range(S)] + [v0[:1] + v1[S:], v0[S:] + v1[:1]]
    eids = [i0[a:a + 1] * nkeys + i1[:S] for a in range(S)] + [i0[:1] * nkeys + i1[S:], i0[S:] * nkeys + i1[:1]]
    cand = jnp.concatenate(vals, axis=0)
    row = lax.broadcasted_iota(jnp.int32, cand.shape, 0)
    slab, b = row >> 3, row & (S - 1)
    flat = jnp.where(slab < S, slab * 16 + b, jnp.where(slab == S, S + b, (S + b) * 16))
    return cand, jnp.concatenate(eids, axis=0), flat.astype(F32)


def _route_kernel(x_ref, sc_ref, sh_ref, wq_ref, sk_ref, h_ref, i_ref, j_ref, g_ref, *, nheads, nkeys, topk):
    h = (x_ref[0] * (1.0 + sc_ref[0]) + sh_ref[0]).astype(BF16)
    h_ref[0] = h
    q = _dot(h, wq_ref[...])
    dk = sk_ref.shape[2]
    i_all, j_all, g_all = [], [], []
    for hd in range(nheads):
        tops = []
        for p in range(2):
            qs = q[:, (2 * hd + p) * dk:(2 * hd + p + 1) * dk].astype(BF16)
            st = _dot_nt(sk_ref[p], qs)
            tops.append(_topk_rows(st, topk))
        (v0, i0), (v1, i1) = tops
        cand, cidx, flat = _pair_candidates(v0, i0, v1, i1, nkeys)
        best, eidx = [], []
        for _ in range(topk):
            m = jnp.max(cand, axis=0, keepdims=True)
            pos = jnp.min(jnp.where(cand == m, flat, float(topk * topk)), axis=0, keepdims=True)
            hit = flat == pos
            best.append(m)
            eidx.append(jnp.sum(jnp.where(hit, cidx, 0), axis=0, keepdims=True))
            cand = jnp.where(hit, -jnp.inf, cand)
        best = jnp.concatenate(best, axis=0)
        e = jnp.concatenate(eidx, axis=0)
        pe = jnp.exp(best - best[0:1])
        g_all.append(pe / jnp.sum(pe, axis=0, keepdims=True))
        i_all.append(e >> (nkeys.bit_length() - 1))
        j_all.append(e & (nkeys - 1))
    i_ref[0] = jnp.concatenate(i_all, axis=0).T
    j_ref[0] = jnp.concatenate(j_all, axis=0).T
    g_ref[0] = jnp.concatenate(g_all, axis=0).T


def peer_route(x, sc, sh, wq, sk, tm):
    B, T, D = x.shape
    mi = _mod_index(sc.shape[0], B)
    npairs = PEER_HEADS * PEER_TOPK
    kern = functools.partial(_route_kernel, nheads=PEER_HEADS, nkeys=sk.shape[1], topk=PEER_TOPK)
    row = lambda w: pl.BlockSpec((1, tm, w), lambda b, t: (b, t, 0))
    return pl.pallas_call(
        kern, grid=(B, T // tm),
        in_specs=[row(D), pl.BlockSpec((1, 1, D), mi), pl.BlockSpec((1, 1, D), mi),
                  pl.BlockSpec(wq.shape, lambda b, t: (0, 0)),
                  pl.BlockSpec(sk.shape, lambda b, t: (0, 0, 0))],
        out_specs=[row(D), row(npairs), row(npairs), row(npairs)],
        out_shape=[jax.ShapeDtypeStruct((B, T, D), BF16),
                   jax.ShapeDtypeStruct((B, T, npairs), jnp.int32),
                   jax.ShapeDtypeStruct((B, T, npairs), jnp.int32),
                   jax.ShapeDtypeStruct((B, T, npairs), F32)],
        compiler_params=_cp(("parallel", "parallel")))(x, sc, sh, wq, sk)


def _peer_act_kernel(h_ref, i_ref, j_ref, g_ref, u_ref, w_ref, a_ref, *, ib, nkeys):
    s_id = pl.program_id(2)

    @pl.when(s_id == 0)
    def _():
        a_ref[...] = jnp.zeros_like(a_ref)

    sc = _dot_nt(h_ref[0], u_ref[...])
    irow = i_ref[0]
    jcol = j_ref[0]
    a = a_ref[...]
    for ii in range(ib):
        got = jnp.take_along_axis(sc[:, ii * nkeys:(ii + 1) * nkeys], jcol, axis=1, mode="promise_in_bounds")
        a = jnp.where(irow == s_id * ib + ii, got, a)
    a_ref[...] = a

    @pl.when(s_id == pl.num_programs(2) - 1)
    def _():
        act = 0.5 * a * (1.0 + lax.erf(a * (2.0 ** -0.5)))
        w_ref[0] = act * g_ref[0]


def peer_act(hb, ii, jj, g, u, tm, ib):
    B, T, D = hb.shape
    npairs = ii.shape[2]
    nkeys = PEER_NKEYS
    steps = u.shape[0] // (ib * nkeys)
    kern = functools.partial(_peer_act_kernel, ib=ib, nkeys=nkeys)
    row = lambda w: pl.BlockSpec((1, tm, w), lambda b, t, s: (b, t, 0))
    return pl.pallas_call(
        kern, grid=(B, T // tm, steps),
        in_specs=[row(D), row(npairs), row(npairs), row(npairs),
                  pl.BlockSpec((ib * nkeys, D), lambda b, t, s: (s, 0))],
        out_specs=row(npairs),
        out_shape=jax.ShapeDtypeStruct((B, T, npairs), F32),
        scratch_shapes=[pltpu.VMEM((tm, npairs), F32)],
        compiler_params=_cp(("parallel", "parallel", "arbitrary")))(hb, ii, jj, g, u)


def _peer_mix_kernel(x_ref, g2_ref, i_ref, j_ref, w_ref, v_ref, lng_ref, lnb_ref, o_ref, p_ref, acc_ref,
                     *, ib, nkeys, pitch):
    s_id = pl.program_id(2)
    tm = x_ref.shape[1]

    @pl.when(s_id == 0)
    def _():
        acc_ref[...] = jnp.zeros_like(acc_ref)
        sub = lax.broadcasted_iota(jnp.int32, (nkeys, i_ref.shape[2]), 0)

        def body(n, carry):
            irow = i_ref[0, pl.ds(n, 1), :]
            jrow = j_ref[0, pl.ds(n, 1), :]
            wrow = w_ref[0, pl.ds(n, 1), :]
            lhs = jnp.where(sub == irow, wrow, 0.0).astype(BF16)
            rhs = jnp.where(sub == jrow, 1.0, 0.0).astype(BF16)
            p_ref[pl.ds(n, nkeys, stride=pitch), :] = _dot_nt(lhs, rhs)
            return carry

        lax.fori_loop(0, tm, body, 0, unroll=MIX_UNROLL)

    parts = [p_ref[pl.ds(pl.multiple_of((s_id * ib + ii) * pitch, 8), tm), :].astype(BF16) for ii in range(ib)]
    acc_ref[...] += _dot(jnp.concatenate(parts, axis=1), v_ref[...])

    @pl.when(s_id == pl.num_programs(2) - 1)
    def _():
        o_ref[0] = _layer_norm(ALPHA * x_ref[0] + g2_ref[0] * acc_ref[...], lng_ref[...], lnb_ref[...])


def peer_mix(x, g2, ii, jj, w, v, lng, lnb, tm, ib):
    B, T, D = x.shape
    npairs = ii.shape[2]
    nkeys = PEER_NKEYS
    steps = v.shape[0] // (ib * nkeys)
    pitch = tm + P_PITCH_PAD
    mi = _mod_index(g2.shape[0], B)
    kern = functools.partial(_peer_mix_kernel, ib=ib, nkeys=nkeys, pitch=pitch)
    row = lambda w_: pl.BlockSpec((1, tm, w_), lambda b, t, s: (b, t, 0))
    c2 = lambda b, t, s: (0, 0)
    return pl.pallas_call(
        kern, grid=(B, T // tm, steps),
        in_specs=[row(D), pl.BlockSpec((1, 1, D), lambda b, t, s: mi(b, t)),
                  row(npairs), row(npairs), row(npairs),
                  pl.BlockSpec((ib * nkeys, D), lambda b, t, s: (s, 0)),
                  pl.BlockSpec(lng.shape, c2), pl.BlockSpec(lnb.shape, c2)],
        out_specs=row(D),
        out_shape=jax.ShapeDtypeStruct((B, T, D), F32),
        scratch_shapes=[pltpu.VMEM((nkeys * pitch, nkeys), F32), pltpu.VMEM((tm, D), F32)],
        compiler_params=_cp(("parallel", "parallel", "arbitrary")))(x, g2, ii, jj, w, v, lng, lnb)


def _filter_features(L):
    t = jnp.linspace(0.0, 1.0, L, dtype=F32)
    w = 2.0 * math.pi * jnp.arange(L, dtype=F32) / L
    bands = jnp.linspace(1e-4, HY_FILTER_BANDS - 1, HY_FILTER_BANDS, dtype=F32)
    z = jnp.concatenate([t[:, None], jnp.cos(w[:, None] * bands[None]), -jnp.sin(w[:, None] * bands[None])], axis=-1)
    z = jnp.pad(z, ((0, 0), (0, LANES - z.shape[1])))
    max_decay = math.log(HY_DECAY_TARGET) / HY_FAST_DECAY
    min_decay = math.log(HY_DECAY_TARGET) / HY_SLOW_DECAY
    deltas = jnp.linspace(min_decay, max_decay, HY_WIDTH, dtype=F32)
    decay = jnp.exp(-t[:, None] * jnp.abs(deltas)[None, :])
    back = lambda a: jnp.concatenate([a[:1], a[:0:-1]], axis=0)
    return z, back(z), decay, back(decay).at[0].set(0.0)


def _hy_filter_kernel(zf_ref, zb_ref, df_ref, db_ref, w1_ref, b1_ref, fr1_ref, w2_ref, b2_ref, fr2_ref,
                      w3f_ref, b3f_ref, w3b_ref, b3b_ref, kf_ref, kb_ref, norm_ref):
    def hidden(z):
        dot = functools.partial(jnp.dot, precision=lax.Precision.HIGHEST, preferred_element_type=F32)
        h = jnp.sin(fr1_ref[...] * (dot(z, w1_ref[...]) + b1_ref[...]))
        return jnp.sin(fr2_ref[...] * (dot(h, w2_ref[...]) + b2_ref[...])), dot

    hf, dot = hidden(zf_ref[...])
    hb, _ = hidden(zb_ref[...])
    kf = (dot(hf, w3f_ref[...]) + b3f_ref[...]) * df_ref[...]
    kb = (dot(hb, w3b_ref[...]) + b3b_ref[...]) * db_ref[...]
    kf_ref[...] = kf
    kb_ref[...] = kb

    @pl.when(pl.program_id(0) == 0)
    def _():
        norm_ref[...] = jnp.zeros_like(norm_ref)

    norm_ref[...] += jnp.sum(jnp.abs(kf) + jnp.abs(kb), axis=0, keepdims=True)


def hy_filter(feats, w1, b1, fr1, w2, b2, fr2, w3, b3):
    zf, zb, df, db = feats
    L, CH = df.shape
    tr = min(1024, L)
    w1p = jnp.pad(w1, ((0, LANES - w1.shape[0]), (0, 0)))
    row = lambda a: pl.BlockSpec((tr, a.shape[1]), lambda i: (i, 0))
    full = lambda a: pl.BlockSpec(a.shape, lambda i: (0, 0))
    args = (zf, zb, df, db, w1p, b1[None, :], fr1[None, :], w2, b2[None, :], fr2[None, :],
            w3[:, :CH], b3[None, :CH], w3[:, CH:], b3[None, CH:])
    return pl.pallas_call(
        _hy_filter_kernel, grid=(L // tr,),
        in_specs=[row(a) for a in args[:4]] + [full(a) for a in args[4:]],
        out_specs=[pl.BlockSpec((tr, CH), lambda i: (i, 0)), pl.BlockSpec((tr, CH), lambda i: (i, 0)),
                   pl.BlockSpec((1, CH), lambda i: (0, 0))],
        out_shape=[jax.ShapeDtypeStruct((L, CH), F32), jax.ShapeDtypeStruct((L, CH), F32),
                   jax.ShapeDtypeStruct((1, CH), F32)],
        compiler_params=_cp(("arbitrary",)))(*args)


def _split_hi_lo(m):
    m32 = jnp.asarray(m, F32)
    hi = m32.astype(BF16)
    return hi, (m32 - hi.astype(F32)).astype(BF16)


def _dot3(m_hi, m_lo, x):
    x_hi = x.astype(BF16)
    x_lo = (x - x_hi.astype(F32)).astype(BF16)
    return _dot(m_hi, x_hi) + _dot(m_hi, x_lo) + _dot(m_lo, x_hi)


def _pad_rows(x, rows):
    if x.shape[0] == rows:
        return x
    return jnp.concatenate([x, jnp.zeros((rows - x.shape[0], x.shape[1]), x.dtype)], axis=0)


def _round_up(n, m):
    return -(-n // m) * m


def _lane_bcast(col):
    return jnp.asarray(np.repeat(np.asarray(col, np.float64).reshape(-1, 1), LANES, axis=1), F32)


def _fft_consts(L):
    N, N2 = 2 * L, HY_N2
    N1 = N // N2
    H1 = N1 // 2
    k1, n1, n2 = np.arange(N1), np.arange(H1), np.arange(N2)
    th = 2 * np.pi * np.outer(k1, n1) / N1
    f1 = np.zeros((2 * N1, _round_up(H1, LANES)))
    f1[:N1, :H1], f1[N1:, :H1] = np.cos(th), -np.sin(th)
    m1 = np.zeros((H1, _round_up(2 * N1, LANES)))
    m1[:, :N1], m1[:, N1:2 * N1] = np.cos(th.T) / N, -np.sin(th.T) / N
    ph = 2 * np.pi * np.outer(n2, n2) / N2
    c, s = np.cos(ph), np.sin(ph)
    m2 = np.block([[c, s], [-s, c]])
    m2i = np.block([[c, -s], [s, c]])
    w = 2 * np.pi / N
    t1 = w * 8 * np.outer(np.arange(N2 // 8), k1)
    t2 = w * np.outer(np.arange(8), k1)
    u1 = w * 8 * np.outer(np.arange(N1 // 8), n2)
    u2 = w * np.outer(np.arange(8), n2)
    tabs = []
    for ang, sign in ((t1, -1.0), (t2, -1.0), (u1, 1.0), (u2, 1.0)):
        tabs += [_lane_bcast(np.cos(ang)), _lane_bcast(sign * np.sin(ang))]
    return dict(N1=N1, mats=_split_hi_lo(f1) + _split_hi_lo(m2) + _split_hi_lo(m2i) + _split_hi_lo(m1), tabs=tuple(tabs))


def _short_conv_block(z, w_ref, b_ref):
    L = z.shape[0]
    row = lax.broadcasted_iota(jnp.int32, z.shape, 0)
    zm = jnp.where(row == 0, 0.0, pltpu.roll(z, 1, axis=0))
    zp = jnp.where(row == L - 1, 0.0, pltpu.roll(z, L - 1, axis=0))
    return zm * w_ref[0:1, :] + z * w_ref[1:2, :] + zp * w_ref[2:3, :] + b_ref[...]


def _cmul(ar, ai, br, bi):
    return ar * br - ai * bi, ar * bi + ai * br


def _fft_stage1(src_ref, as_ref, f1h, f1l, t1r, t1i, t2r, t2i, *, N1, N2):
    H1 = N1 // 2
    pa = 2 * N1 + FFT_PITCH_PAD
    kp = f1h.shape[1]

    C = src_ref.shape[1]

    def body(i, carry):
        n2s = [i * FFT_TIME_GROUP + g for g in range(FFT_TIME_GROUP)]
        xs = jnp.concatenate([src_ref[pl.ds(n2, H1, stride=N2), :] for n2 in n2s], axis=1)
        a = _dot3(f1h[...], f1l[...], _pad_rows(xs, kp))
        for g, n2 in enumerate(n2s):
            ag = a[:, g * C:(g + 1) * C]
            oa = pl.multiple_of((n2 >> 3) * N1, SUBLANES)
            ob = pl.multiple_of((n2 & 7) * N1, SUBLANES)
            tr, ti = _cmul(t1r[pl.ds(oa, N1), :], t1i[pl.ds(oa, N1), :], t2r[pl.ds(ob, N1), :], t2i[pl.ds(ob, N1), :])
            yr, yi = _cmul(ag[:N1], ag[N1:], tr, ti)
            base = pl.multiple_of(n2 * pa, SUBLANES)
            as_ref[pl.ds(base, N1), :] = yr
            as_ref[pl.ds(base + N1, N1), :] = yi
        return carry

    lax.fori_loop(0, N2 // FFT_TIME_GROUP, body, 0, unroll=FFT_UNROLL)


def _fft_stage2(as_ref, k1s, m2h, m2l, *, N1, N2):
    pa = 2 * N1 + FFT_PITCH_PAD
    v = jnp.concatenate(
        [jnp.concatenate([as_ref[pl.ds(k1, N2, stride=pa), :], as_ref[pl.ds(N1 + k1, N2, stride=pa), :]], axis=0)
         for k1 in k1s], axis=1)
    return _dot3(m2h[...], m2l[...], v)


def _hy_spec_kernel(kf_ref, kb_ref, norm_ref, f1h, f1l, m2h, m2l, t1r, t1i, t2r, t2i, o_ref, as_ref, *, N1, N2):
    C = kf_ref.shape[1]
    inv = 1.0 / norm_ref[...]
    for part, src in enumerate((kf_ref, kb_ref)):
        _fft_stage1(src, as_ref, f1h, f1l, t1r, t1i, t2r, t2i, N1=N1, N2=N2)

        def body(i, carry):
            k1s = [i * FFT_FREQ_GROUP + g for g in range(FFT_FREQ_GROUP)]
            x = _fft_stage2(as_ref, k1s, m2h, m2l, N1=N1, N2=N2)
            for g, k1 in enumerate(k1s):
                xg = x[:, g * C:(g + 1) * C] * inv
                if part == 0:
                    o_ref[k1] = xg
                else:
                    o_ref[k1] = o_ref[k1] + (1.0 - 2.0 * (k1 & 1)) * xg
            return carry

        lax.fori_loop(0, N1 // FFT_FREQ_GROUP, body, 0)


def hy_spectrum(kf, kb, norm, fc):
    L, CH = kf.shape
    N1, N2, C = fc["N1"], HY_N2, HY_CB
    f1h, f1l, m2h, m2l = fc["mats"][:4]
    tabs = fc["tabs"][:4]
    kern = functools.partial(_hy_spec_kernel, N1=N1, N2=N2)
    const = lambda a: pl.BlockSpec(a.shape, lambda j: (0, 0))
    return pl.pallas_call(
        kern, grid=(CH // C,),
        in_specs=[pl.BlockSpec((L, C), lambda j: (0, j)), pl.BlockSpec((L, C), lambda j: (0, j)),
                  pl.BlockSpec((1, C), lambda j: (0, j))]
        + [const(a) for a in (f1h, f1l, m2h, m2l) + tuple(tabs)],
        out_specs=pl.BlockSpec((N1, 2 * N2, C), lambda j: (0, 0, j)),
        out_shape=jax.ShapeDtypeStruct((N1, 2 * N2, CH), F32),
        scratch_shapes=[pltpu.VMEM((N2 * (2 * N1 + FFT_PITCH_PAD), C), F32)],
        compiler_params=_cp(("parallel",)))(kf, kb, norm, f1h, f1l, m2h, m2l, *tabs)


def _hy_conv_kernel(x0_ref, x1_ref, v_ref, w0_ref, w1_ref, wv_ref, b0_ref, b1_ref, bv_ref, hb_ref, ksp_ref,
                    f1h, f1l, m2h, m2l, m2ih, m2il, m1h, m1l, t1r, t1i, t2r, t2i, u1r, u1i, u2r, u2i,
                    o_ref, uv_ref, g_ref, as_ref, bs_ref, *, N1, N2):
    H1 = N1 // 2
    pb = 2 * N2 + FFT_PITCH_PAD
    uv_ref[...] = _short_conv_block(v_ref[0], wv_ref, bv_ref) * _short_conv_block(x1_ref[0], w1_ref, b1_ref)
    g_ref[...] = _short_conv_block(x0_ref[0], w0_ref, b0_ref)
    _fft_stage1(uv_ref, as_ref, f1h, f1l, t1r, t1i, t2r, t2i, N1=N1, N2=N2)

    C = uv_ref.shape[1]

    def freq_body(i, carry):
        k1s = [i * FFT_FREQ_GROUP + g for g in range(FFT_FREQ_GROUP)]
        x = _fft_stage2(as_ref, k1s, m2h, m2l, N1=N1, N2=N2)
        ks = jnp.concatenate([ksp_ref[k1] for k1 in k1s], axis=1)
        yr, yi = _cmul(x[:N2], x[N2:], ks[:N2], ks[N2:])
        b = _dot3(m2ih[...], m2il[...], jnp.concatenate([yr, yi], axis=0))
        for g, k1 in enumerate(k1s):
            bg = b[:, g * C:(g + 1) * C]
            oa = pl.multiple_of((k1 >> 3) * N2, SUBLANES)
            ob = pl.multiple_of((k1 & 7) * N2, SUBLANES)
            ur, ui = _cmul(u1r[pl.ds(oa, N2), :], u1i[pl.ds(oa, N2), :], u2r[pl.ds(ob, N2), :], u2i[pl.ds(ob, N2), :])
            br, bi = _cmul(bg[:N2], bg[N2:], ur, ui)
            base = pl.multiple_of(k1 * pb, SUBLANES)
            bs_ref[pl.ds(base, N2), :] = br
            bs_ref[pl.ds(base + N2, N2), :] = bi
        return carry

    lax.fori_loop(0, N1 // FFT_FREQ_GROUP, freq_body, 0, unroll=FFT_UNROLL)
    kp = m1h.shape[1]

    def time_body(i, carry):
        n2s = [i * FFT_TIME_GROUP + g for g in range(FFT_TIME_GROUP)]
        w = jnp.concatenate(
            [_pad_rows(jnp.concatenate([bs_ref[pl.ds(n2, N1, stride=pb), :], bs_ref[pl.ds(N2 + n2, N1, stride=pb), :]],
                                       axis=0), kp) for n2 in n2s], axis=1)
        y = _dot3(m1h[...], m1l[...], w)
        for g, n2 in enumerate(n2s):
            rows = pl.ds(n2, H1, stride=N2)
            o_ref[0, rows, :] = (y[:, g * C:(g + 1) * C] + uv_ref[rows, :] * hb_ref[...]) * g_ref[rows, :]
        return carry

    lax.fori_loop(0, N2 // FFT_TIME_GROUP, time_body, 0, unroll=FFT_UNROLL)


def hy_conv(zh, short_w, short_b, hy_bias, ksp, fc):
    B, L, W3 = zh.shape
    CH = W3 // 3
    N1, N2, C = fc["N1"], HY_N2, HY_CB
    nb = CH // C
    kern = functools.partial(_hy_conv_kernel, N1=N1, N2=N2)
    sync = pl.Buffered(1)
    zspec = lambda g: pl.BlockSpec((1, L, C), lambda j, b: (b, 0, g * nb + j), pipeline_mode=sync)
    wspec = lambda g: pl.BlockSpec((HY_SHORT, C), lambda j, b: (0, g * nb + j))
    bspec = lambda g: pl.BlockSpec((1, C), lambda j, b: (0, g * nb + j))
    const = lambda a: pl.BlockSpec(a.shape, lambda j, b: (0, 0), pipeline_mode=sync)
    consts = fc["mats"] + fc["tabs"]
    return pl.pallas_call(
        kern, grid=(nb, B),
        in_specs=[zspec(0), zspec(1), zspec(2), wspec(0), wspec(1), wspec(2), bspec(0), bspec(1), bspec(2),
                  pl.BlockSpec((1, C), lambda j, b: (0, j)),
                  pl.BlockSpec((N1, 2 * N2, C), lambda j, b: (0, 0, j), pipeline_mode=sync)]
        + [const(a) for a in consts],
        out_specs=pl.BlockSpec((1, L, C), lambda j, b: (b, 0, j)),
        out_shape=jax.ShapeDtypeStruct((B, L, CH), F32),
        scratch_shapes=[pltpu.VMEM((L, C), F32), pltpu.VMEM((L, C), F32),
                        pltpu.VMEM((N2 * (2 * N1 + FFT_PITCH_PAD), C), F32),
                        pltpu.VMEM((N1 * (2 * N2 + FFT_PITCH_PAD), C), F32)],
        compiler_params=_cp(("parallel", "parallel")))(
            zh, zh, zh, short_w, short_w, short_w, short_b, short_b, short_b, hy_bias, ksp, *consts)


def _dft_consts(L):
    N = 2 * L
    k = np.arange(N)
    ang = 2 * np.pi * np.outer(k, k) / N
    fwd = np.concatenate([np.cos(ang), -np.sin(ang)], axis=0)
    inv = np.concatenate([np.cos(ang[:L]), -np.sin(ang[:L])], axis=1) / N
    return _split_hi_lo(fwd[:, :L]) + _split_hi_lo(fwd) + _split_hi_lo(inv)


def _hy_small_kernel(x0_ref, x1_ref, v_ref, w0_ref, w1_ref, wv_ref, b0_ref, b1_ref, bv_ref, hb_ref,
                     kf_ref, kb_ref, norm_ref, fh, fl, ffh, ffl, ih, il, o_ref):
    N = 2 * kf_ref.shape[0]
    uv = _short_conv_block(v_ref[0], wv_ref, bv_ref) * _short_conv_block(x1_ref[0], w1_ref, b1_ref)
    g = _short_conv_block(x0_ref[0], w0_ref, b0_ref)
    x = _dot3(fh[...], fl[...], uv)
    kfull = jnp.concatenate([kf_ref[...], kb_ref[...]], axis=0) * (1.0 / norm_ref[...])
    ks = _dot3(ffh[...], ffl[...], kfull)
    yr, yi = _cmul(x[:N], x[N:], ks[:N], ks[N:])
    y = _dot3(ih[...], il[...], jnp.concatenate([yr, yi], axis=0))
    o_ref[0] = (y + uv * hb_ref[...]) * g


def hy_conv_small(zh, short_w, short_b, hy_bias, kf, kb, norm, dc):
    B, L, W3 = zh.shape
    CH = W3 // 3
    C = HY_CB
    nb = CH // C
    zspec = lambda g: pl.BlockSpec((1, L, C), lambda j, b: (b, 0, g * nb + j))
    wspec = lambda g: pl.BlockSpec((HY_SHORT, C), lambda j, b: (0, g * nb + j))
    bspec = lambda g: pl.BlockSpec((1, C), lambda j, b: (0, g * nb + j))
    const = lambda a: pl.BlockSpec(a.shape, lambda j, b: (0, 0))
    return pl.pallas_call(
        _hy_small_kernel, grid=(nb, B),
        in_specs=[zspec(0), zspec(1), zspec(2), wspec(0), wspec(1), wspec(2), bspec(0), bspec(1), bspec(2),
                  pl.BlockSpec((1, C), lambda j, b: (0, j)),
                  pl.BlockSpec((L, C), lambda j, b: (0, j)), pl.BlockSpec((L, C), lambda j, b: (0, j)),
                  pl.BlockSpec((1, C), lambda j, b: (0, j))]
        + [const(a) for a in dc],
        out_specs=pl.BlockSpec((1, L, C), lambda j, b: (b, 0, j)),
        out_shape=jax.ShapeDtypeStruct((B, L, CH), F32),
        compiler_params=_cp(("parallel", "parallel")))(
            zh, zh, zh, short_w, short_w, short_w, short_b, short_b, short_b, hy_bias, kf, kb, norm, *dc)


def _gqa_slot_offsets():
    group = GQA_HEADS // GQA_KV_HEADS
    return [(h * LANES + (h // group) * HEAD_DIM) for h in range(GQA_HEADS)]


def _pad_gqa_cols(w):
    out = jnp.zeros((w.shape[0], GQA_HEADS * LANES), w.dtype)
    for h, off in enumerate(_gqa_slot_offsets()):
        out = out.at[:, off:off + HEAD_DIM].set(w[:, h * HEAD_DIM:(h + 1) * HEAD_DIM])
    return out


def _pad_gqa_vec(g):
    out = jnp.zeros((GQA_HEADS * LANES,), g.dtype)
    for off in _gqa_slot_offsets():
        out = out.at[off:off + HEAD_DIM].set(g)
    return out


def _rope_consts(T):
    half = HEAD_DIM // 2
    quarter = half // 2
    t = jnp.arange(T)
    inv = ROPE_THETA ** (-jnp.arange(0, half, 2, dtype=F32) / half)
    ar = (t // GRID_W).astype(F32)[:, None] * inv[None, :]
    ac = (t % GRID_W).astype(F32)[:, None] * inv[None, :]
    ang = jnp.concatenate([ar, ar, ac, ac] * (LANES // HEAD_DIM), axis=1)
    P = np.zeros((LANES, LANES), np.float32)
    for o in range(0, LANES, half):
        for j in range(quarter):
            P[o + j + quarter, o + j] = -1.0
            P[o + j, o + j + quarter] = 1.0
    return jnp.cos(ang), jnp.sin(ang), jnp.asarray(P, BF16)


def _head_mean_mat():
    m = np.zeros((LANES, LANES), np.float32)
    for s in range(0, LANES, HEAD_DIM):
        m[s:s + HEAD_DIM, s:s + HEAD_DIM] = 1.0 / HEAD_DIM
    return jnp.asarray(m, BF16)


def _na_bias_table(rpb, rows):
    wr = min(NA_WIN_ROWS, rows)
    wc = NA_WIN_COLS
    q = np.arange(GRID_W)
    c0 = np.clip(q - wc // 2, 0, GRID_W - wc)
    kc = np.arange(GRID_W)
    valid = (kc[None, :] >= c0[:, None]) & (kc[None, :] < c0[:, None] + wc)
    dc = np.clip(kc[None, :] - q[:, None] + (NA_WIN_COLS - 1), 0, 2 * NA_WIN_COLS - 2)
    assert wr == NA_WIN_ROWS
    H, ncols = rpb.shape[0], rpb.shape[2]
    onehot = (dc[None] == np.arange(ncols)[:, None, None]) & valid[None]
    tq = jnp.einsum('hdc,cqk->hdqk', rpb, jnp.asarray(onehot, F32), precision=lax.Precision.HIGHEST)
    tq = jnp.where(jnp.asarray(valid)[None, None], tq, NEG)
    slabs = [jnp.transpose(tq[:, o:o + wr], (0, 2, 1, 3)).reshape(H, GRID_W, wr * GRID_W) for o in range(wr)]
    return jnp.stack(slabs, axis=0)


def _tile_plan(T):
    return dict(proj=min(512, T), attn=min(256, T), merge=min(256, T), route=min(256, T),
                act=min(2048, T), mix=min(256, T))


def kernel(x, c, ctx, c_ctx, w_ada, b_ada, w_in, na_rpb, gqa_q_gain, gqa_k_gain, hy_short_w, hy_short_b,
           hy_f_w1, hy_f_b1, hy_f_freq1, hy_f_w2, hy_f_b2, hy_f_freq2, hy_f_w3, hy_f_b3, hy_bias,
           w_br_a, w_br_b, w_br_c, w_out, ln1_g, ln1_b, peer_w_q, peer_sub_keys, peer_u, peer_v, ln2_g, ln2_b):
    B, T, D = x.shape
    C = ctx.shape[1]
    depth = w_ada.shape[0]
    rows = T // GRID_W
    wr = min(NA_WIN_ROWS, rows)
    NAW = NA_HEADS * HEAD_DIM
    GQW = GQA_HEADS * HEAD_DIM
    GKW = GQA_KV_HEADS * HEAD_DIM
    o_na, o_gq, o_gk, o_gv, o_hy, o_gl = 0, 3 * NAW, 3 * NAW + GQW, 3 * NAW + GQW + GKW, 3 * NAW + GQW + 2 * GKW, \
        3 * NAW + GQW + 2 * GKW + 3 * HY_WIDTH

    cos_t, sin_t, rope_perm = _rope_consts(T)
    head_mean = _head_mean_mat()
    fft_c = _fft_consts(T)
    feats_t = _filter_features(T)
    feats_c = _filter_features(C)
    dft_c = _dft_consts(C)
    c_all = jnp.concatenate([c, c_ctx[None, :], jnp.zeros((16 - B - 1, D), F32)], axis=0)
    tp = _tile_plan(T)
    tm, tmc = tp["proj"], C

    for l in range(depth):
        need_ctx = l < depth - 1
        mod = ada_mod(c_all, w_ada[l], b_ada[l][None, :])
        ml = mod[:B].reshape(B, 1, 6, D)
        mc = mod[B:B + 1].reshape(1, 1, 6, D)
        sh1, sc1, g1, sh2, sc2, g2 = [ml[:, :, i] for i in range(6)]
        csh1, csc1, cg1, csh2, csc2, cg2 = [mc[:, :, i] for i in range(6)]

        wl = w_in[l]
        w_na = wl[:, o_na:o_gq].astype(BF16)
        w_gq = jnp.concatenate([_pad_gqa_cols(wl[:, o_gq:o_gk]), wl[:, o_gk:o_hy]], axis=1).astype(BF16)
        w_hy = wl[:, o_hy:o_gl].astype(BF16)
        w_gl = wl[:, o_gl:].astype(BF16)
        gqk = jnp.concatenate([_pad_gqa_vec(gqa_q_gain[l]), jnp.tile(gqa_k_gain[l], GQA_KV_HEADS)])[None, :]
        wa = w_br_a[l].astype(BF16)
        wb = _pad_gqa_cols(w_br_b[l].T).T.astype(BF16)
        wc = w_br_c[l].astype(BF16)
        wo = w_out[l].astype(BF16)
        lng1, lnb1 = ln1_g[l][None, :], ln1_b[l][None, :]
        lng2, lnb2 = ln2_g[l][None, :], ln2_b[l][None, :]
        wq = peer_w_q[l].astype(BF16)
        sk = peer_sub_keys[l].astype(BF16)
        ub = peer_u[l].astype(BF16)
        vb = peer_v[l].astype(BF16)
        bias_tab = _na_bias_table(na_rpb[l], rows)

        qkv_a = mod_matmul(x, sc1, sh1, w_na, BF16, tm)
        qkv_ac = mod_matmul(ctx, csc1, csh1, w_na, BF16, tmc)
        ya = na_attention(qkv_a, qkv_ac, bias_tab, GRID_W, wr)

        qb, kb, vb_ = gqa_proj(x, sc1, sh1, w_gq, gqk, head_mean, rope_perm, cos_t, sin_t, True, tm)
        qbc, kbc, vbc = gqa_proj(ctx, csc1, csh1, w_gq, gqk, head_mean, rope_perm, cos_t[:C], sin_t[:C], False, tmc)
        yb = gqa_attention(qb, jnp.concatenate([kb, kbc], axis=1), jnp.concatenate([vb_, vbc], axis=1), tp["attn"])

        zh = mod_matmul(x, sc1, sh1, w_hy, F32, tm)
        filt = functools.partial(hy_filter, w1=hy_f_w1[l], b1=hy_f_b1[l], fr1=hy_f_freq1[l], w2=hy_f_w2[l],
                                 b2=hy_f_b2[l], fr2=hy_f_freq2[l], w3=hy_f_w3[l], b3=hy_f_b3[l])
        ksp = hy_spectrum(*filt(feats_t), fft_c)
        yc = hy_conv(zh, hy_short_w[l], hy_short_b[l][None, :], hy_bias[l][None, :], ksp, fft_c)

        x_mid = merge_block(x, sc1, sh1, g1, ya, yb, yc, w_gl, wa, wb, wc, wo, lng1, lnb1, tp["merge"])

        hb, ii, jj, gg = peer_route(x_mid, sc2, sh2, wq, sk, tp["route"])
        wgt = peer_act(hb, ii, jj, gg, ub, tp["act"], ACT_KEY_ROWS)
        x_new = peer_mix(x_mid, g2, ii, jj, wgt, vb, lng2, lnb2, tp["mix"], MIX_KEY_ROWS)

        if need_ctx:
            yac = ctx_pair_attention(qkv_ac)
            ybc = gqa_attention(qbc, kbc, vbc, C)
            zhc = mod_matmul(ctx, csc1, csh1, w_hy, F32, tmc)
            ycc = hy_conv_small(zhc, hy_short_w[l], hy_short_b[l][None, :], hy_bias[l][None, :],
                                *filt(feats_c), dft_c)
            c_mid = merge_block(ctx, csc1, csh1, cg1, yac, ybc, ycc, w_gl, wa, wb, wc, wo, lng1, lnb1, tmc)
            hbc, iic, jjc, ggc = peer_route(c_mid, csc2, csh2, wq, sk, tmc)
            wgtc = peer_act(hbc, iic, jjc, ggc, ub, tmc, ACT_KEY_ROWS)
            ctx = peer_mix(c_mid, cg2, iic, jjc, wgtc, vb, lng2, lnb2, tmc, MIX_KEY_ROWS)
        x = x_new
    return x
```
